```python
import jax, jax.numpy as jnp
from jax import lax
import numpy as np

D_MODEL = 2048
BATCH = 2
SEQ = 8192
DEPTH = 4

CHUNK = 64
EPS = 1e-6
POOL_WIDTH = 1024
POOL_WINDOWS = (2, 4, 8, 16)
N_POOL_GROUPS = 4
POOL_GROUP = POOL_WIDTH // N_POOL_GROUPS
N_Q_HEADS = 16
N_KV_HEADS = 4
HEAD_DIM = 64
ATTN_WIDTH = N_Q_HEADS * HEAD_DIM
KV_WIDTH = N_KV_HEADS * HEAD_DIM
WINDOW = 128
WINDOW_CHUNKS = WINDOW // CHUNK
CONV_WIDTH = 1024
CONV_KERNEL = 31
N_BRANCHES = 3
IN_SPLITS = (POOL_WIDTH, POOL_WIDTH, ATTN_WIDTH, KV_WIDTH, KV_WIDTH, ATTN_WIDTH,
             CONV_WIDTH, CONV_WIDTH, CONV_WIDTH, N_BRANCHES * D_MODEL)
IN_WIDTH = 2 * POOL_WIDTH + 2 * ATTN_WIDTH + 2 * KV_WIDTH + 3 * CONV_WIDTH + N_BRANCHES * D_MODEL

kernel_name = "hybrid_pool_swa_conformer_parallel"


def rms_norm(x, g):
    xf = x.astype(jnp.float32)
    y = xf * lax.rsqrt(jnp.mean(xf * xf, axis=-1, keepdims=True) + EPS)
    return y.astype(x.dtype) * g


def layer_norm(x, g, b):
    xf = x.astype(jnp.float32)
    mu = jnp.mean(xf, axis=-1, keepdims=True)
    var = jnp.mean(jnp.square(xf - mu), axis=-1, keepdims=True)
    return ((xf - mu) * lax.rsqrt(var + EPS)).astype(x.dtype) * g + b


def multiscale_pool(u, pool_w, pool_scale):
    B, S, _ = u.shape
    ug = u.astype(jnp.float32).reshape(B, S, N_POOL_GROUPS, POOL_GROUP)
    cs = jnp.cumsum(ug, axis=1)
    t = jnp.arange(S)
    means = []
    for gi, w in enumerate(POOL_WINDOWS):
        c_g = cs[:, :, gi]
        prev = jnp.pad(c_g, ((0, 0), (w, 0), (0, 0)))[:, :S]
        cnt = jnp.minimum(t + 1, w).astype(jnp.float32)[None, :, None]
        means.append((c_g - prev) / cnt)
    mixed = (jnp.stack(means, axis=2) - ug).astype(u.dtype)
    y = jnp.einsum('bsgc,gcd->bsgd', mixed, pool_w)
    return y.reshape(B, S, POOL_WIDTH) * pool_scale


def window_attention(q, k, v, sink):
    B, S = q.shape[:2]
    nc = S // CHUNK
    grp = N_Q_HEADS // N_KV_HEADS
    qc = q.reshape(B, nc, CHUNK, N_KV_HEADS, grp, HEAD_DIM)
    pad = WINDOW_CHUNKS * CHUNK

    def band(t):
        tp = jnp.pad(t, ((0, 0), (pad, 0), (0, 0), (0, 0)))
        tp = tp.reshape(B, nc + WINDOW_CHUNKS, CHUNK, N_KV_HEADS, HEAD_DIM)
        return jnp.concatenate([tp[:, i:i + nc] for i in range(WINDOW_CHUNKS + 1)], axis=2)

    kb, vb = band(k), band(v)
    s = jnp.einsum('bnqhgd,bnkhd->bnhgqk', qc, kb).astype(jnp.float32) * (HEAD_DIM ** -0.5)
    n_keys = (WINDOW_CHUNKS + 1) * CHUNK
    key_chunk = jnp.arange(nc)[:, None] - WINDOW_CHUNKS + jnp.arange(n_keys)[None, :] // CHUNK
    valid = (key_chunk >= 0)[None, :, None, None, None, :]
    s = jnp.where(valid, s, -jnp.inf)
    sk = sink.astype(jnp.float32).reshape(1, 1, N_KV_HEADS, grp, 1, 1)
    m = jnp.maximum(jnp.max(s, axis=-1, keepdims=True), sk)
    p = jnp.exp(s - m)
    denom = jnp.sum(p, axis=-1, keepdims=True) + jnp.exp(sk - m)
    p = (p / denom).astype(v.dtype)
    o = jnp.einsum('bnhgqk,bnkhd->bnqhgd', p, vb)
    return o.reshape(B, S, ATTN_WIDTH)


def conformer_conv(a, b, dw, dw_b, ln_g, ln_b, pw):
    g = a * jax.nn.sigmoid(b)
    gp = jnp.pad(g, ((0, 0), (CONV_KERNEL - 1, 0), (0, 0)))
    y = lax.conv_general_dilated(gp, dw[:, None, :], window_strides=(1,), padding='VALID',
                                 dimension_numbers=('NWC', 'WIO', 'NWC'),
                                 feature_group_count=CONV_WIDTH) + dw_b
    y = jax.nn.silu(layer_norm(y, ln_g, ln_b))
    return y @ pw


def setup_inputs(seed: int = 0) -> dict:
    key = jax.random.key(seed)
    ks = jax.random.split(key, 20)
    f32 = jnp.float32

    def nrm(k, shape, scale):
        return jax.random.normal(k, shape, f32) * scale

    L, D = DEPTH, D_MODEL
    return {
        "x": nrm(ks[0], (BATCH, SEQ, D), 1.0),
        "c": nrm(ks[1], (BATCH, D), 1.0),
        "norm_g": 1.0 + nrm(ks[2], (L, D), 0.05),
        "w_ada": nrm(ks[3], (L, D, 3 * D), 0.5 * D ** -0.5),
        "b_ada": nrm(ks[4], (L, 3 * D), 0.01),
        "w_in": nrm(ks[5], (L, D, IN_WIDTH), D ** -0.5),
        "pool_w": nrm(ks[6], (L, N_POOL_GROUPS, POOL_GROUP, POOL_GROUP), POOL_GROUP ** -0.5),
        "pool_scale": 1.0 + nrm(ks[7], (L, POOL_WIDTH), 0.1),
        "attn_sink": nrm(ks[8], (L, N_Q_HEADS), 1.0),
        "conv_dw": nrm(ks[9], (L, CONV_KERNEL, CONV_WIDTH), CONV_KERNEL ** -0.5),
        "conv_dw_b": nrm(ks[10], (L, CONV_WIDTH), 0.01),
        "conv_ln_g": 1.0 + nrm(ks[11], (L, CONV_WIDTH), 0.05),
        "conv_ln_b": nrm(ks[12], (L, CONV_WIDTH), 0.01),
        "conv_pw": nrm(ks[13], (L, CONV_WIDTH, CONV_WIDTH), CONV_WIDTH ** -0.5),
        "w_branch_pool": nrm(ks[14], (L, POOL_WIDTH, D), POOL_WIDTH ** -0.5),
        "w_branch_attn": nrm(ks[15], (L, ATTN_WIDTH, D), ATTN_WIDTH ** -0.5),
        "w_branch_conv": nrm(ks[16], (L, CONV_WIDTH, D), CONV_WIDTH ** -0.5),
        "w_out": nrm(ks[17], (L, D, D), D ** -0.5),
        "final_g": 1.0 + nrm(ks[18], (D,), 0.05),
    }


def reference(x, c, norm_g, w_ada, b_ada, w_in, pool_w, pool_scale, attn_sink, conv_dw, conv_dw_b,
              conv_ln_g, conv_ln_b, conv_pw, w_branch_pool, w_branch_attn, w_branch_conv, w_out,
              final_g):
    B, S, _ = x.shape
    split_idx = np.cumsum(IN_SPLITS)[:-1].tolist()
    c_act = jax.nn.silu(c)
    for l in range(DEPTH):
        mod = c_act @ w_ada[l] + b_ada[l]
        shift, scale, gate = jnp.split(mod, 3, axis=-1)
        h = rms_norm(x, norm_g[l]) * (1 + scale[:, None]) + shift[:, None]
        proj = h @ w_in[l]
        pool_u, pool_z, q, k, v, attn_z, conv_a, conv_b, conv_z, gates = jnp.split(proj, split_idx, axis=-1)
        y_pool = multiscale_pool(pool_u, pool_w[l], pool_scale[l]) * jax.nn.silu(pool_z)
        y_attn = window_attention(q.reshape(B, S, N_Q_HEADS, HEAD_DIM),
                                  k.reshape(B, S, N_KV_HEADS, HEAD_DIM),
                                  v.reshape(B, S, N_KV_HEADS, HEAD_DIM),
                                  attn_sink[l]) * jax.nn.silu(attn_z)
        y_conv = conformer_conv(conv_a, conv_b, conv_dw[l], conv_dw_b[l], conv_ln_g[l], conv_ln_b[l],
                                conv_pw[l]) * jax.nn.silu(conv_z)
        g_pool, g_attn, g_conv = jnp.split(jax.nn.sigmoid(gates), 3, axis=-1)
        merged = (g_pool * (y_pool @ w_branch_pool[l])
                  + g_attn * (y_attn @ w_branch_attn[l])
                  + g_conv * (y_conv @ w_branch_conv[l]))
        x = x + gate[:, None] * (merged @ w_out[l])
    return rms_norm(x, final_g)
```

```python
import functools

import jax
import jax.numpy as jnp
from jax import lax
from jax.experimental import pallas as pl
from jax.experimental.pallas import tpu as pltpu

D_MODEL = 2048
CHUNK = 64
EPS = 1e-6
POOL_WIDTH = 1024
POOL_WINDOWS = (2, 4, 8, 16)
N_POOL_GROUPS = 4
POOL_GROUP = POOL_WIDTH // N_POOL_GROUPS
N_Q_HEADS = 16
N_KV_HEADS = 4
HEAD_DIM = 64
GROUP = N_Q_HEADS // N_KV_HEADS
ATTN_WIDTH = N_Q_HEADS * HEAD_DIM
KV_WIDTH = N_KV_HEADS * HEAD_DIM
WINDOW_CHUNKS = 2
N_KEYS = (WINDOW_CHUNKS + 1) * CHUNK
CONV_WIDTH = 1024
CONV_KERNEL = 31
IN_WIDTH = 2 * POOL_WIDTH + 2 * ATTN_WIDTH + 2 * KV_WIDTH + 3 * CONV_WIDTH + 3 * D_MODEL

GATE_COL = 0
POOL_U_COL = 6
POOL_Z_COL = 7
Q_COL = 8
ATTN_Z_COL = 9
CONV_A_COL = 10
CONV_B_COL = 11
CONV_Z_COL = 12
K_COL = 52
V_COL = 53

POOL_HALO = 16
CONV_HALO = 32
ATTN_HALO = WINDOW_CHUNKS * CHUNK
SUBLANES = 8

VMEM_LIMIT = 56 * 1024 * 1024

F32 = jnp.float32
BF16 = jnp.bfloat16


def _sigmoid(v):
    return jax.nn.sigmoid(v)


def _silu(v):
    return v * jax.nn.sigmoid(v)


def _ada_kernel(c_ref, w_ref, b_ref, o_ref):
    ca = _silu(c_ref[...])
    o_ref[...] = jnp.dot(ca, w_ref[...], preferred_element_type=F32,
                         precision=lax.Precision.HIGHEST) + b_ref[...]


def _ada_mod(c_pad, w_ada, b_ada):
    depth, d, n = w_ada.shape
    tn = 1024
    return pl.pallas_call(
        _ada_kernel,
        out_shape=jax.ShapeDtypeStruct((depth, c_pad.shape[0], n), F32),
        grid=(depth, n // tn),
        in_specs=[
            pl.BlockSpec(c_pad.shape, lambda l, j: (0, 0)),
            pl.BlockSpec((None, d, tn), lambda l, j: (l, 0, j)),
            pl.BlockSpec((None, 1, tn), lambda l, j: (l, 0, j)),
        ],
        out_specs=pl.BlockSpec((None, c_pad.shape[0], tn), lambda l, j: (l, 0, j)),
        compiler_params=pltpu.CompilerParams(dimension_semantics=("parallel", "parallel")),
        name="ada_mod",
    )(c_pad, w_ada, b_ada.reshape(depth, 1, n))


def _inproj_kernel(x_ref, g_ref, sc_ref, sh_ref, w_ref, o_ref, h_ref, *, rows):
    @pl.when(pl.program_id(1) == 0)
    def _():
        g = g_ref[...]
        sc = 1.0 + sc_ref[...]
        sh = sh_ref[...]

        def body(r, carry):
            r0 = pl.multiple_of(r * rows, rows)
            xr = x_ref[pl.ds(r0, rows), :]
            ms = jnp.mean(xr * xr, axis=-1, keepdims=True)
            y = xr * lax.rsqrt(ms + EPS) * g
            h_ref[pl.ds(r0, rows), :] = (y * sc + sh).astype(BF16)
            return carry

        lax.fori_loop(0, x_ref.shape[0] // rows, body, 0)

    o_ref[...] = jnp.dot(h_ref[...], w_ref[...], preferred_element_type=F32).astype(BF16)


def _in_proj(xf, norm_g, scale, shift, w_in, layer, seq):
    m, d = xf.shape
    n = w_in.shape[-1]
    tm, tn = 1024, 1536
    return pl.pallas_call(
        functools.partial(_inproj_kernel, rows=128),
        out_shape=jax.ShapeDtypeStruct((m, n), BF16),
        grid=(m // tm, n // tn),
        in_specs=[
            pl.BlockSpec((tm, d), lambda i, j: (i, 0)),
            pl.BlockSpec((None, 1, d), lambda i, j: (layer, 0, 0)),
            pl.BlockSpec((None, 1, d), lambda i, j: ((i * tm) // seq, 0, 0)),
            pl.BlockSpec((None, 1, d), lambda i, j: ((i * tm) // seq, 0, 0)),
            pl.BlockSpec((None, d, tn), lambda i, j: (layer, 0, j)),
        ],
        out_specs=pl.BlockSpec((tm, tn), lambda i, j: (i, j)),
        scratch_shapes=[pltpu.VMEM((tm, d), BF16)],
        compiler_params=pltpu.CompilerParams(
            dimension_semantics=("parallel", "arbitrary"), vmem_limit_bytes=VMEM_LIMIT),
        name="in_proj",
    )(xf, norm_g, scale, shift, w_in)


def _pool_kernel(u_ref, uh_ref, z_ref, w_ref, s_ref, o_ref, *, tiles_per_seq):
    ts = u_ref.shape[0]
    ti = pl.program_id(0) % tiles_per_seq
    hal_all = jnp.where(ti == 0, 0.0, uh_ref[...].astype(F32))
    t = ti * ts + lax.broadcasted_iota(jnp.int32, (ts, POOL_GROUP), 0)
    for gi, w in enumerate(POOL_WINDOWS):
        cols = slice(gi * POOL_GROUP, (gi + 1) * POOL_GROUP)
        ug = u_ref[:, cols].astype(F32)
        s = jnp.concatenate([hal_all[:, cols], ug], axis=0)
        k = 1
        while k < w:
            s = s[k:] + s[:-k]
            k *= 2
        off = POOL_HALO - (w - 1)
        s = s[off:off + ts]
        cnt = jnp.minimum(t + 1, w).astype(F32)
        mixed = (s / cnt - ug).astype(BF16)
        y = jnp.dot(mixed, w_ref[gi], preferred_element_type=F32)
        y = y * s_ref[:, cols] * _silu(z_ref[:, cols].astype(F32))
        o_ref[:, cols] = y.astype(BF16)


def _pool_mixer(proj, pool_w, pool_scale, layer, seq):
    m = proj.shape[0]
    ts = 512
    hb = ts // POOL_HALO
    return pl.pallas_call(
        functools.partial(_pool_kernel, tiles_per_seq=seq // ts),
        out_shape=jax.ShapeDtypeStruct((m, POOL_WIDTH), BF16),
        grid=(m // ts,),
        in_specs=[
            pl.BlockSpec((ts, POOL_WIDTH), lambda i: (i, POOL_U_COL)),
            pl.BlockSpec((POOL_HALO, POOL_WIDTH), lambda i: (jnp.maximum(i * hb - 1, 0), POOL_U_COL)),
            pl.BlockSpec((ts, POOL_WIDTH), lambda i: (i, POOL_Z_COL)),
            pl.BlockSpec((None, N_POOL_GROUPS, POOL_GROUP, POOL_GROUP), lambda i: (layer, 0, 0, 0)),
            pl.BlockSpec((None, 1, POOL_WIDTH), lambda i: (layer, 0, 0)),
        ],
        out_specs=pl.BlockSpec((ts, POOL_WIDTH), lambda i: (i, 0)),
        compiler_params=pltpu.CompilerParams(
            dimension_semantics=("parallel",), vmem_limit_bytes=VMEM_LIMIT),
        name="pool_mixer",
    )(proj, proj, proj, pool_w, pool_scale)


def _attn_kernel(sink_ref, q_ref, k_ref, kh_ref, v_ref, vh_ref, z_ref, o_ref, kx_ref, vx_ref,
                 *, tiles_per_seq, layer):
    ts = q_ref.shape[0]
    ti = pl.program_id(0) % tiles_per_seq
    kx_ref[0:ATTN_HALO, :] = kh_ref[...]
    kx_ref[ATTN_HALO:, :] = k_ref[...]
    vx_ref[0:ATTN_HALO, :] = vh_ref[...]
    vx_ref[ATTN_HALO:, :] = v_ref[...]
    row_head = lax.broadcasted_iota(jnp.int32, (GROUP * CHUNK, 1), 0) // CHUNK
    key_chunk = lax.broadcasted_iota(jnp.int32, (1, N_KEYS), 1) // CHUNK

    def chunk_body(c, carry):
        r0 = pl.multiple_of(c * CHUNK, CHUNK)
        qc = q_ref[pl.ds(r0, CHUNK), :]
        kc = kx_ref[pl.ds(r0, N_KEYS), :]
        vc = vx_ref[pl.ds(r0, N_KEYS), :]
        n = ti * (ts // CHUNK) + c
        bias = jnp.where(n - WINDOW_CHUNKS + key_chunk >= 0, 0.0, -jnp.inf).astype(F32)
        pieces = []
        for kvh in range(N_KV_HEADS):
            heads = [kvh * GROUP + g for g in range(GROUP)]
            qs = jnp.concatenate([qc[:, h * HEAD_DIM:(h + 1) * HEAD_DIM] for h in heads], axis=0)
            kk = kc[:, kvh * HEAD_DIM:(kvh + 1) * HEAD_DIM]
            vv = vc[:, kvh * HEAD_DIM:(kvh + 1) * HEAD_DIM]
            s = lax.dot_general(qs, kk, (((1,), (1,)), ((), ())), preferred_element_type=F32)
            s = s * (HEAD_DIM ** -0.5) + bias
            sk = jnp.full((GROUP * CHUNK, 1), sink_ref[layer, heads[GROUP - 1]], F32)
            for g in range(GROUP - 2, -1, -1):
                sk = jnp.where(row_head == g, sink_ref[layer, heads[g]], sk)
            mx = jnp.maximum(jnp.max(s, axis=-1, keepdims=True), sk)
            p = jnp.exp(s - mx)
            denom = jnp.sum(p, axis=-1, keepdims=True) + jnp.exp(sk - mx)
            o = jnp.dot(p.astype(BF16), vv, preferred_element_type=F32) / denom
            pieces.extend(o[g * CHUNK:(g + 1) * CHUNK] for g in range(GROUP))
        oc = jnp.concatenate(pieces, axis=1)
        zc = z_ref[pl.ds(r0, CHUNK), :].astype(F32)
        o_ref[pl.ds(r0, CHUNK), :] = (oc * _silu(zc)).astype(BF16)
        return carry

    lax.fori_loop(0, ts // CHUNK, chunk_body, 0)


def _attn_mixer(proj, attn_sink, layer, seq):
    m = proj.shape[0]
    ts = 512
    hb = ts // ATTN_HALO
    halo_map_k = lambda i: (jnp.maximum(i * hb - 1, 0), K_COL)
    halo_map_v = lambda i: (jnp.maximum(i * hb - 1, 0), V_COL)
    return pl.pallas_call(
        functools.partial(_attn_kernel, tiles_per_seq=seq // ts, layer=layer),
        out_shape=jax.ShapeDtypeStruct((m, ATTN_WIDTH), BF16),
        grid=(m // ts,),
        in_specs=[
            pl.BlockSpec(memory_space=pltpu.SMEM),
            pl.BlockSpec((ts, ATTN_WIDTH), lambda i: (i, Q_COL)),
            pl.BlockSpec((ts, KV_WIDTH), lambda i: (i, K_COL)),
            pl.BlockSpec((ATTN_HALO, KV_WIDTH), halo_map_k),
            pl.BlockSpec((ts, KV_WIDTH), lambda i: (i, V_COL)),
            pl.BlockSpec((ATTN_HALO, KV_WIDTH), halo_map_v),
            pl.BlockSpec((ts, ATTN_WIDTH), lambda i: (i, ATTN_Z_COL)),
        ],
        out_specs=pl.BlockSpec((ts, ATTN_WIDTH), lambda i: (i, 0)),
        scratch_shapes=[pltpu.VMEM((ts + ATTN_HALO, KV_WIDTH), BF16),
                        pltpu.VMEM((ts + ATTN_HALO, KV_WIDTH), BF16)],
        compiler_params=pltpu.CompilerParams(
            dimension_semantics=("parallel",), vmem_limit_bytes=VMEM_LIMIT),
        name="attn_mixer",
    )(attn_sink, proj, proj, proj, proj, proj, proj)


def _conv_kernel(a_ref, ah_ref, b_ref, bh_ref, z_ref, dw_ref, dwb_ref, lg_ref, lb_ref, pw_ref, o_ref,
                 gs_ref, y_ref, *, tiles_per_seq, rows, lanes):
    ts = a_ref.shape[0]
    ext = ts + CONV_HALO
    ti = pl.program_id(0) % tiles_per_seq
    gh = ah_ref[...].astype(F32) * _sigmoid(bh_ref[...].astype(F32))
    gs_ref[0, 0:CONV_HALO, :] = jnp.where(ti == 0, 0.0, gh)
    gs_ref[0, CONV_HALO:, :] = a_ref[...].astype(F32) * _sigmoid(b_ref[...].astype(F32))
    for s in range(1, SUBLANES):
        gs_ref[s, 0:ext - SUBLANES, :] = gs_ref[0, s:s + ext - SUBLANES, :]
    base = CONV_HALO - (CONV_KERNEL - 1)

    def row_body(r, carry):
        r0 = pl.multiple_of(r * rows, rows)
        for lc in range(CONV_WIDTH // lanes):
            cols = slice(lc * lanes, (lc + 1) * lanes)
            acc = jnp.zeros((rows, lanes), F32)
            for j in range(CONV_KERNEL):
                q, s = divmod(base + j, SUBLANES)
                start = pl.multiple_of(r0 + q * SUBLANES, SUBLANES)
                acc = acc + dw_ref[j:j + 1, cols] * gs_ref[s, pl.ds(start, rows), cols]
            y_ref[pl.ds(r0, rows), cols] = acc + dwb_ref[:, cols]
        return carry

    lax.fori_loop(0, ts // rows, row_body, 0)

    y = y_ref[...]
    mu = jnp.mean(y, axis=-1, keepdims=True)
    yc = y - mu
    var = jnp.mean(yc * yc, axis=-1, keepdims=True)
    yn = yc * lax.rsqrt(var + EPS) * lg_ref[...] + lb_ref[...]
    act = _silu(yn).astype(BF16)
    out = jnp.dot(act, pw_ref[...], preferred_element_type=F32)
    o_ref[...] = (out * _silu(z_ref[...].astype(F32))).astype(BF16)


def _conv_mixer(proj, conv_dw, conv_dw_b, conv_ln_g, conv_ln_b, conv_pw, layer, seq):
    m = proj.shape[0]
    ts = 512
    hb = ts // CONV_HALO
    vec = pl.BlockSpec((None, 1, CONV_WIDTH), lambda i: (layer, 0, 0))
    return pl.pallas_call(
        functools.partial(_conv_kernel, tiles_per_seq=seq // ts, rows=32, lanes=512),
        out_shape=jax.ShapeDtypeStruct((m, CONV_WIDTH), BF16),
        grid=(m // ts,),
        in_specs=[
            pl.BlockSpec((ts, CONV_WIDTH), lambda i: (i, CONV_A_COL)),
            pl.BlockSpec((CONV_HALO, CONV_WIDTH), lambda i: (jnp.maximum(i * hb - 1, 0), CONV_A_COL)),
            pl.BlockSpec((ts, CONV_WIDTH), lambda i: (i, CONV_B_COL)),
            pl.BlockSpec((CONV_HALO, CONV_WIDTH), lambda i: (jnp.maximum(i * hb - 1, 0), CONV_B_COL)),
            pl.BlockSpec((ts, CONV_WIDTH), lambda i: (i, CONV_Z_COL)),
            pl.BlockSpec((None, CONV_KERNEL, CONV_WIDTH), lambda i: (layer, 0, 0)),
            vec, vec, vec,
            pl.BlockSpec((None, CONV_WIDTH, CONV_WIDTH), lambda i: (layer, 0, 0)),
        ],
        out_specs=pl.BlockSpec((ts, CONV_WIDTH), lambda i: (i, 0)),
        scratch_shapes=[pltpu.VMEM((SUBLANES, ts + CONV_HALO, CONV_WIDTH), F32),
                        pltpu.VMEM((ts, CONV_WIDTH), F32)],
        compiler_params=pltpu.CompilerParams(
            dimension_semantics=("parallel",), vmem_limit_bytes=VMEM_LIMIT),
        name="conv_mixer",
    )(proj, proj, proj, proj, proj, conv_dw, conv_dw_b, conv_ln_g, conv_ln_b, conv_pw)


def _out_kernel(x_ref, yp_ref, ya_ref, yc_ref, gp_ref, ga_ref, gc_ref, wp_ref, wa_ref, wc_ref,
                wo_ref, gate_ref, o_ref):
    def branch(y_ref, g_ref, w_ref):
        return _sigmoid(g_ref[...].astype(F32)) * jnp.dot(
            y_ref[...], w_ref[...], preferred_element_type=F32)

    merged = branch(yp_ref, gp_ref, wp_ref) + branch(ya_ref, ga_ref, wa_ref) + branch(yc_ref, gc_ref, wc_ref)
    upd = jnp.dot(merged.astype(BF16), wo_ref[...], preferred_element_type=F32)
    o_ref[...] = x_ref[...] + gate_ref[...] * upd


def _merge_out(xf, proj, y_pool, y_attn, y_conv, w_bp, w_ba, w_bc, w_out, gate, layer, seq):
    m, d = xf.shape
    tm = 256
    once = pl.Buffered(1)
    ybs = pl.BlockSpec((tm, POOL_WIDTH), lambda i: (i, 0))
    wbs = pl.BlockSpec((None, POOL_WIDTH, d), lambda i: (layer, 0, 0), pipeline_mode=once)
    return pl.pallas_call(
        _out_kernel,
        out_shape=jax.ShapeDtypeStruct((m, d), F32),
        grid=(m // tm,),
        in_specs=[
            pl.BlockSpec((tm, d), lambda i: (i, 0)),
            ybs, ybs, ybs,
            pl.BlockSpec((tm, d), lambda i: (i, GATE_COL)),
            pl.BlockSpec((tm, d), lambda i: (i, GATE_COL + 1)),
            pl.BlockSpec((tm, d), lambda i: (i, GATE_COL + 2)),
            wbs, wbs, wbs,
            pl.BlockSpec((None, d, d), lambda i: (layer, 0, 0), pipeline_mode=once),
            pl.BlockSpec((None, 1, d), lambda i: ((i * tm) // seq, 0, 0)),
        ],
        out_specs=pl.BlockSpec((tm, d), lambda i: (i, 0)),
        compiler_params=pltpu.CompilerParams(
            dimension_semantics=("parallel",), vmem_limit_bytes=VMEM_LIMIT),
        name="merge_out",
    )(xf, y_pool, y_attn, y_conv, proj, proj, proj, w_bp, w_ba, w_bc, w_out, gate)


def _final_norm_kernel(x_ref, g_ref, o_ref):
    x = x_ref[...]
    ms = jnp.mean(x * x, axis=-1, keepdims=True)
    o_ref[...] = x * lax.rsqrt(ms + EPS) * g_ref[...]


def _final_norm(xf, final_g):
    m, d = xf.shape
    tm = 512
    return pl.pallas_call(
        _final_norm_kernel,
        out_shape=jax.ShapeDtypeStruct((m, d), F32),
        grid=(m // tm,),
        in_specs=[pl.BlockSpec((tm, d), lambda i: (i, 0)),
                  pl.BlockSpec((1, d), lambda i: (0, 0))],
        out_specs=pl.BlockSpec((tm, d), lambda i: (i, 0)),
        compiler_params=pltpu.CompilerParams(dimension_semantics=("parallel",)),
        name="final_norm",
    )(xf, final_g.reshape(1, d))


def kernel(x, c, norm_g, w_ada, b_ada, w_in, pool_w, pool_scale, attn_sink, conv_dw, conv_dw_b,
           conv_ln_g, conv_ln_b, conv_pw, w_branch_pool, w_branch_attn, w_branch_conv, w_out,
           final_g):
    batch, seq, d = x.shape
    depth = w_in.shape[0]
    m = batch * seq
    assert d == D_MODEL and w_in.shape[-1] == IN_WIDTH and seq % 1024 == 0 and batch <= 8

    kv0 = 2 * POOL_WIDTH + ATTN_WIDTH
    kv1 = kv0 + 2 * KV_WIDTH
    g0 = IN_WIDTH - 3 * D_MODEL
    w_in_r = jnp.concatenate(
        [w_in[..., g0:], w_in[..., :kv0], w_in[..., kv1:g0], w_in[..., kv0:kv1]], axis=-1).astype(BF16)
    pool_w_b = pool_w.astype(BF16)
    conv_pw_b = conv_pw.astype(BF16)
    w_bp = w_branch_pool.astype(BF16)
    w_ba = w_branch_attn.astype(BF16)
    w_bc = w_branch_conv.astype(BF16)
    w_o = w_out.astype(BF16)

    c_pad = jnp.zeros((8, d), F32).at[:batch].set(c)
    mod = _ada_mod(c_pad, w_ada, b_ada)

    norm_g3 = norm_g.reshape(depth, 1, d)
    pool_scale3 = pool_scale.reshape(depth, 1, POOL_WIDTH)
    dwb3 = conv_dw_b.reshape(depth, 1, CONV_WIDTH)
    lng3 = conv_ln_g.reshape(depth, 1, CONV_WIDTH)
    lnb3 = conv_ln_b.reshape(depth, 1, CONV_WIDTH)

    xf = x.reshape(m, d)
    for layer in range(depth):
        shift = mod[layer, :batch, 0:d].reshape(batch, 1, d)
        scale = mod[layer, :batch, d:2 * d].reshape(batch, 1, d)
        gate = mod[layer, :batch, 2 * d:3 * d].reshape(batch, 1, d)
        proj = _in_proj(xf, norm_g3, scale, shift, w_in_r, layer, seq)
        y_pool = _pool_mixer(proj, pool_w_b, pool_scale3, layer, seq)
        y_attn = _attn_mixer(proj, attn_sink, layer, seq)
        y_conv = _conv_mixer(proj, conv_dw, dwb3, lng3, lnb3, conv_pw_b, layer, seq)
        xf = _merge_out(xf, proj, y_pool, y_attn, y_conv, w_bp, w_ba, w_bc, w_o, gate, layer, seq)
    return _final_norm(xf, final_g).reshape(batch, seq, d)
```

```python
import functools

import jax
import jax.numpy as jnp
from jax import lax
from jax.experimental import pallas as pl
from jax.experimental.pallas import tpu as pltpu

D_MODEL = 2048
CHUNK = 64
EPS = 1e-6
POOL_WIDTH = 1024
POOL_WINDOWS = (2, 4, 8, 16)
N_POOL_GROUPS = 4
POOL_GROUP = POOL_WIDTH // N_POOL_GROUPS
N_Q_HEADS = 16
N_KV_HEADS = 4
HEAD_DIM = 64
GROUP = N_Q_HEADS // N_KV_HEADS
ATTN_WIDTH = N_Q_HEADS * HEAD_DIM
KV_WIDTH = N_KV_HEADS * HEAD_DIM
WINDOW_CHUNKS = 2
N_KEYS = (WINDOW_CHUNKS + 1) * CHUNK
CONV_WIDTH = 1024
CONV_KERNEL = 31
IN_WIDTH = 2 * POOL_WIDTH + 2 * ATTN_WIDTH + 2 * KV_WIDTH + 3 * CONV_WIDTH + 3 * D_MODEL

POOL_COL = 0
Q_COL = 2 * POOL_WIDTH
KV_COL = Q_COL + ATTN_WIDTH
ATTN_Z_COL = KV_COL + 2 * KV_WIDTH
CONV_COL = ATTN_Z_COL + ATTN_WIDTH
GATE_COL = CONV_COL + 3 * CONV_WIDTH
GATE_BLOCK = 512
CONV_BLOCK = 1536

POOL_HALO = 16
CONV_HALO = 32
ATTN_HALO = WINDOW_CHUNKS * CHUNK
SUBLANES = 8

BRANCH_TM = 256
MERGE_TM = 256
VMEM_LIMIT = 56 * 1024 * 1024

F32 = jnp.float32
BF16 = jnp.bfloat16


def _sigmoid(v):
    return jax.nn.sigmoid(v)


def _silu(v):
    return v * jax.nn.sigmoid(v)


def _dot(a, b):
    return jnp.dot(a, b, preferred_element_type=F32)


def _ada_norm(x, g, scale, shift):
    ms = jnp.mean(x * x, axis=-1, keepdims=True)
    return x * lax.rsqrt(ms + EPS) * g * (1.0 + scale) + shift


def _ada_kernel(c_ref, w_ref, b_ref, o_ref):
    ca = _silu(c_ref[...])
    o_ref[...] = jnp.dot(ca, w_ref[...], preferred_element_type=F32,
                         precision=lax.Precision.HIGHEST) + b_ref[...]


def _ada_mod(c_pad, w_ada, b_ada):
    depth, d, n = w_ada.shape
    tn = 1024
    return pl.pallas_call(
        _ada_kernel,
        out_shape=jax.ShapeDtypeStruct((depth, c_pad.shape[0], n), F32),
        grid=(depth, n // tn),
        in_specs=[
            pl.BlockSpec(c_pad.shape, lambda l, j: (0, 0)),
            pl.BlockSpec((None, d, tn), lambda l, j: (l, 0, j)),
            pl.BlockSpec((None, 1, tn), lambda l, j: (l, 0, j)),
        ],
        out_specs=pl.BlockSpec((None, c_pad.shape[0], tn), lambda l, j: (l, 0, j)),
        compiler_params=pltpu.CompilerParams(dimension_semantics=("parallel", "parallel")),
        name="ada_mod",
    )(c_pad, w_ada, b_ada.reshape(depth, 1, n))


def _norm_kernel(x_ref, g_ref, sc_ref, sh_ref, h_ref):
    h_ref[...] = _ada_norm(x_ref[...], g_ref[...], sc_ref[...], sh_ref[...]).astype(BF16)


def _first_norm(xf, norm_g, scale, shift, seq):
    m, d = xf.shape
    tm = 256
    vec = lambda i: ((i * tm) // seq, 0, 0)
    return pl.pallas_call(
        _norm_kernel,
        out_shape=jax.ShapeDtypeStruct((m, d), BF16),
        grid=(m // tm,),
        in_specs=[pl.BlockSpec((tm, d), lambda i: (i, 0)),
                  pl.BlockSpec((None, 1, d), lambda i: (0, 0, 0)),
                  pl.BlockSpec((None, 1, d), vec),
                  pl.BlockSpec((None, 1, d), vec)],
        out_specs=pl.BlockSpec((tm, d), lambda i: (i, 0)),
        compiler_params=pltpu.CompilerParams(dimension_semantics=("parallel",)),
        name="first_norm",
    )(xf, norm_g, scale, shift)


def _gate_step(h_ref, wg_ref, gate_ref, k):
    gate_ref[:, k * GATE_BLOCK:(k + 1) * GATE_BLOCK] = _sigmoid(_dot(h_ref[...], wg_ref[...])).astype(BF16)


def _pool_branch_kernel(h_ref, wp_ref, wg0, wg1, wg2, wg3, pw_ref, ps_ref, y_ref, gate_ref, carry_ref,
                        *, tiles_per_seq):
    tm = h_ref.shape[0]
    ti = pl.program_id(0) % tiles_per_seq

    @pl.when(pl.program_id(0) == 0)
    def _():
        carry_ref[...] = jnp.zeros(carry_ref.shape, F32)

    t = ti * tm + lax.broadcasted_iota(jnp.int32, (tm, POOL_GROUP), 0)
    for gi, w in enumerate(POOL_WINDOWS):
        cols = slice(gi * POOL_GROUP, (gi + 1) * POOL_GROUP)
        zcols = slice(POOL_WIDTH + gi * POOL_GROUP, POOL_WIDTH + (gi + 1) * POOL_GROUP)
        ug = _dot(h_ref[...], wp_ref[:, cols])
        zg = _dot(h_ref[...], wp_ref[:, zcols])
        halo = jnp.where(ti == 0, 0.0, carry_ref[:, cols])
        carry_ref[:, cols] = ug[tm - POOL_HALO:, :]
        s = jnp.concatenate([halo, ug], axis=0)
        k = 1
        while k < w:
            s = s[k:] + s[:-k]
            k *= 2
        off = POOL_HALO - (w - 1)
        s = s[off:off + tm]
        cnt = jnp.minimum(t + 1, w).astype(F32)
        mixed = (s / cnt - ug).astype(BF16)
        y = _dot(mixed, pw_ref[gi]) * ps_ref[:, cols] * _silu(zg)
        y_ref[:, cols] = y.astype(BF16)
        _gate_step(h_ref, (wg0, wg1, wg2, wg3)[gi], gate_ref, gi)


def _attn_branch_kernel(sink_ref, h_ref, wq_ref, wkv_ref, wz0_ref, wz1_ref, wg0, wg1, wg2, wg3,
                        y_ref, gate_ref, kvx_ref, *, tiles_per_seq, layer):
    tm = h_ref.shape[0]
    ti = pl.program_id(0) % tiles_per_seq
    head_cols = GROUP * HEAD_DIM

    @pl.when(pl.program_id(0) == 0)
    def _():
        kvx_ref[0:ATTN_HALO, :] = jnp.zeros((ATTN_HALO, 2 * KV_WIDTH), BF16)

    kvx_ref[ATTN_HALO:, :] = _dot(h_ref[...], wkv_ref[...]).astype(BF16)
    row_head = lax.broadcasted_iota(jnp.int32, (GROUP * CHUNK, 1), 0) // CHUNK
    key_chunk = lax.broadcasted_iota(jnp.int32, (1, N_KEYS), 1) // CHUNK
    for kvh in range(N_KV_HEADS):
        cols = slice(kvh * head_cols, (kvh + 1) * head_cols)
        q = _dot(h_ref[...], wq_ref[:, cols]).astype(BF16)
        wz_ref = wz0_ref if kvh < N_KV_HEADS // 2 else wz1_ref
        zoff = (kvh % (N_KV_HEADS // 2)) * head_cols
        sz = _silu(_dot(h_ref[...], wz_ref[:, zoff:zoff + head_cols]))
        heads = [kvh * GROUP + g for g in range(GROUP)]
        sk = jnp.full((GROUP * CHUNK, 1), sink_ref[layer, heads[GROUP - 1]], F32)
        for g in range(GROUP - 2, -1, -1):
            sk = jnp.where(row_head == g, sink_ref[layer, heads[g]], sk)
        for c in range(tm // CHUNK):
            r0 = c * CHUNK
            qc = q[r0:r0 + CHUNK, :]
            qs = jnp.concatenate([qc[:, g * HEAD_DIM:(g + 1) * HEAD_DIM] for g in range(GROUP)], axis=0)
            kk = kvx_ref[r0:r0 + N_KEYS, kvh * HEAD_DIM:(kvh + 1) * HEAD_DIM]
            vv = kvx_ref[r0:r0 + N_KEYS, KV_WIDTH + kvh * HEAD_DIM:KV_WIDTH + (kvh + 1) * HEAD_DIM]
            n = ti * (tm // CHUNK) + c
            bias = jnp.where(n - WINDOW_CHUNKS + key_chunk >= 0, 0.0, -jnp.inf).astype(F32)
            s = lax.dot_general(qs, kk, (((1,), (1,)), ((), ())), preferred_element_type=F32)
            s = s * (HEAD_DIM ** -0.5) + bias
            mx = jnp.maximum(jnp.max(s, axis=-1, keepdims=True), sk)
            p = jnp.exp(s - mx)
            denom = jnp.sum(p, axis=-1, keepdims=True) + jnp.exp(sk - mx)
            o = _dot(p.astype(BF16), vv) / denom
            oc = jnp.concatenate([o[g * CHUNK:(g + 1) * CHUNK] for g in range(GROUP)], axis=1)
            y_ref[r0:r0 + CHUNK, cols] = (oc * sz[r0:r0 + CHUNK, :]).astype(BF16)
        _gate_step(h_ref, (wg0, wg1, wg2, wg3)[kvh], gate_ref, kvh)
    kvx_ref[0:ATTN_HALO, :] = kvx_ref[tm:tm + ATTN_HALO, :]


def _conv_branch_kernel(h_ref, w0_ref, w1_ref, wg0, wg1, wg2, wg3, dw_ref, dwb_ref, lg_ref, lb_ref,
                        pw_ref, y_ref, gate_ref, gs_ref, carry_ref, yscr_ref, zs_ref,
                        *, tiles_per_seq, rows, cb):
    tm = h_ref.shape[0]
    ext = tm + CONV_HALO
    base = CONV_HALO - (CONV_KERNEL - 1)
    ti = pl.program_id(0) % tiles_per_seq

    @pl.when(pl.program_id(0) == 0)
    def _():
        carry_ref[...] = jnp.zeros(carry_ref.shape, F32)

    def wcols(c0):
        if c0 < CONV_BLOCK:
            return w0_ref[:, c0:c0 + cb]
        return w1_ref[:, c0 - CONV_BLOCK:c0 - CONV_BLOCK + cb]

    for k in range(CONV_WIDTH // cb):
        cols = slice(k * cb, (k + 1) * cb)
        a = _dot(h_ref[...], wcols(k * cb))
        b = _dot(h_ref[...], wcols(CONV_WIDTH + k * cb))
        z = _dot(h_ref[...], wcols(2 * CONV_WIDTH + k * cb))
        g = a * _sigmoid(b)
        gs_ref[0, 0:CONV_HALO, cols] = jnp.where(ti == 0, 0.0, carry_ref[:, cols])
        gs_ref[0, CONV_HALO:, cols] = g
        carry_ref[:, cols] = g[tm - CONV_HALO:, :]
        zs_ref[:, cols] = _silu(z)
        for s in range(1, SUBLANES):
            gs_ref[s, 0:ext - SUBLANES, cols] = gs_ref[0, s:s + ext - SUBLANES, cols]
        for rc in range(tm // rows):
            acc = jnp.zeros((rows, cb), F32)
            for j in range(CONV_KERNEL):
                q, s = divmod(base + j, SUBLANES)
                r0 = rc * rows + q * SUBLANES
                acc = acc + dw_ref[j:j + 1, cols] * gs_ref[s, r0:r0 + rows, cols]
            yscr_ref[rc * rows:(rc + 1) * rows, cols] = acc + dwb_ref[:, cols]
        _gate_step(h_ref, (wg0, wg1, wg2, wg3)[k], gate_ref, k)
    y = yscr_ref[...]
    mu = jnp.mean(y, axis=-1, keepdims=True)
    yc = y - mu
    var = jnp.mean(yc * yc, axis=-1, keepdims=True)
    yn = yc * lax.rsqrt(var + EPS) * lg_ref[...] + lb_ref[...]
    out = _dot(_silu(yn).astype(BF16), pw_ref[...])
    y_ref[...] = (out * zs_ref[...]).astype(BF16)


def _branch_specs(m, d, layer, gate_index):
    tm = BRANCH_TM
    once = pl.Buffered(1)

    def w_spec(width, col):
        assert col % width == 0
        return pl.BlockSpec((None, d, width), lambda i: (layer, 0, col // width), pipeline_mode=once)

    def row_spec(width):
        return pl.BlockSpec((tm, width), lambda i: (i, 0))

    gate_col = GATE_COL + gate_index * D_MODEL
    gate_w_specs = [w_spec(GATE_BLOCK, gate_col + k * GATE_BLOCK) for k in range(D_MODEL // GATE_BLOCK)]
    out_shape = (jax.ShapeDtypeStruct((m, POOL_WIDTH), BF16), jax.ShapeDtypeStruct((m, d), BF16))
    params = pltpu.CompilerParams(dimension_semantics=("arbitrary",), vmem_limit_bytes=VMEM_LIMIT)
    return tm, once, w_spec, row_spec, gate_w_specs, out_shape, params


def _pool_branch(h, w_in, pool_w, pool_scale, layer, seq):
    m, d = h.shape
    tm, once, w_spec, row_spec, gate_w_specs, out_shape, params = _branch_specs(m, d, layer, 0)
    return pl.pallas_call(
        functools.partial(_pool_branch_kernel, tiles_per_seq=seq // tm),
        out_shape=out_shape,
        grid=(m // tm,),
        in_specs=[row_spec(d), w_spec(2 * POOL_WIDTH, POOL_COL)] + gate_w_specs + [
            pl.BlockSpec((None, N_POOL_GROUPS, POOL_GROUP, POOL_GROUP), lambda i: (layer, 0, 0, 0),
                         pipeline_mode=once),
            pl.BlockSpec((None, 1, POOL_WIDTH), lambda i: (layer, 0, 0)),
        ],
        out_specs=(row_spec(POOL_WIDTH), row_spec(d)),
        scratch_shapes=[pltpu.VMEM((POOL_HALO, POOL_WIDTH), F32)],
        compiler_params=params,
        name="pool_branch",
    )(h, w_in, w_in, w_in, w_in, w_in, pool_w, pool_scale)


def _attn_branch(h, w_in, attn_sink, layer, seq):
    m, d = h.shape
    tm, once, w_spec, row_spec, gate_w_specs, out_shape, params = _branch_specs(m, d, layer, 1)
    half = ATTN_WIDTH // 2
    return pl.pallas_call(
        functools.partial(_attn_branch_kernel, tiles_per_seq=seq // tm, layer=layer),
        out_shape=out_shape,
        grid=(m // tm,),
        in_specs=[pl.BlockSpec(memory_space=pltpu.SMEM), row_spec(d),
                  w_spec(ATTN_WIDTH, Q_COL), w_spec(2 * KV_WIDTH, KV_COL),
                  w_spec(half, ATTN_Z_COL), w_spec(half, ATTN_Z_COL + half)] + gate_w_specs,
        out_specs=(row_spec(ATTN_WIDTH), row_spec(d)),
        scratch_shapes=[pltpu.VMEM((tm + ATTN_HALO, 2 * KV_WIDTH), BF16)],
        compiler_params=params,
        name="attn_branch",
    )(attn_sink, h, w_in, w_in, w_in, w_in, w_in, w_in, w_in, w_in)


def _conv_branch(h, w_in, conv_dw, conv_dw_b, conv_ln_g, conv_ln_b, conv_pw, layer, seq):
    m, d = h.shape
    tm, once, w_spec, row_spec, gate_w_specs, out_shape, params = _branch_specs(m, d, layer, 2)
    vec = pl.BlockSpec((None, 1, CONV_WIDTH), lambda i: (layer, 0, 0))
    return pl.pallas_call(
        functools.partial(_conv_branch_kernel, tiles_per_seq=seq // tm, rows=64, cb=256),
        out_shape=out_shape,
        grid=(m // tm,),
        in_specs=[row_spec(d), w_spec(CONV_BLOCK, CONV_COL), w_spec(CONV_BLOCK, CONV_COL + CONV_BLOCK)]
        + gate_w_specs + [
            pl.BlockSpec((None, CONV_KERNEL, CONV_WIDTH), lambda i: (layer, 0, 0)),
            vec, vec, vec,
            pl.BlockSpec((None, CONV_WIDTH, CONV_WIDTH), lambda i: (layer, 0, 0), pipeline_mode=once),
        ],
        out_specs=(row_spec(CONV_WIDTH), row_spec(d)),
        scratch_shapes=[pltpu.VMEM((SUBLANES, tm + CONV_HALO, CONV_WIDTH), F32),
                        pltpu.VMEM((CONV_HALO, CONV_WIDTH), F32),
                        pltpu.VMEM((tm, CONV_WIDTH), F32),
                        pltpu.VMEM((tm, CONV_WIDTH), F32)],
        compiler_params=params,
        name="conv_branch",
    )(h, w_in, w_in, w_in, w_in, w_in, w_in, conv_dw, conv_dw_b, conv_ln_g, conv_ln_b, conv_pw)


def _merged_update(x_ref, yp_ref, ya_ref, yc_ref, gp_ref, ga_ref, gc_ref, wp_ref, wa_ref, wc_ref,
                   wo_ref, gate_ref):
    def branch(y_ref, g_ref, w_ref):
        return g_ref[...].astype(F32) * _dot(y_ref[...], w_ref[...])

    merged = branch(yp_ref, gp_ref, wp_ref) + branch(ya_ref, ga_ref, wa_ref) + branch(yc_ref, gc_ref, wc_ref)
    return x_ref[...] + gate_ref[...] * _dot(merged.astype(BF16), wo_ref[...])


def _merge_kernel(x_ref, yp_ref, ya_ref, yc_ref, gp_ref, ga_ref, gc_ref, wp_ref, wa_ref, wc_ref,
                  wo_ref, gate_ref, ng_ref, nsc_ref, nsh_ref, o_ref, h_ref):
    xn = _merged_update(x_ref, yp_ref, ya_ref, yc_ref, gp_ref, ga_ref, gc_ref, wp_ref, wa_ref, wc_ref,
                        wo_ref, gate_ref)
    o_ref[...] = xn
    h_ref[...] = _ada_norm(xn, ng_ref[...], nsc_ref[...], nsh_ref[...]).astype(BF16)


def _merge_last_kernel(x_ref, yp_ref, ya_ref, yc_ref, gp_ref, ga_ref, gc_ref, wp_ref, wa_ref, wc_ref,
                       wo_ref, gate_ref, fg_ref, o_ref):
    xn = _merged_update(x_ref, yp_ref, ya_ref, yc_ref, gp_ref, ga_ref, gc_ref, wp_ref, wa_ref, wc_ref,
                        wo_ref, gate_ref)
    ms = jnp.mean(xn * xn, axis=-1, keepdims=True)
    o_ref[...] = xn * lax.rsqrt(ms + EPS) * fg_ref[...]


def _merge_out(xf, ys, gates, w_bp, w_ba, w_bc, w_out, gate, layer, seq, nxt):
    m, d = xf.shape
    tm = MERGE_TM
    once = pl.Buffered(1)
    row = pl.BlockSpec((tm, d), lambda i: (i, 0))
    ybs = pl.BlockSpec((tm, POOL_WIDTH), lambda i: (i, 0))
    wbs = pl.BlockSpec((None, POOL_WIDTH, d), lambda i: (layer, 0, 0), pipeline_mode=once)
    per_batch = pl.BlockSpec((None, 1, d), lambda i: ((i * tm) // seq, 0, 0))
    in_specs = [row, ybs, ybs, ybs, row, row, row, wbs, wbs, wbs,
                pl.BlockSpec((None, d, d), lambda i: (layer, 0, 0), pipeline_mode=once), per_batch]
    args = [xf, *ys, *gates, w_bp, w_ba, w_bc, w_out, gate]
    params = pltpu.CompilerParams(dimension_semantics=("parallel",), vmem_limit_bytes=VMEM_LIMIT)
    if len(nxt) == 1:
        return pl.pallas_call(
            _merge_last_kernel,
            out_shape=jax.ShapeDtypeStruct((m, d), F32),
            grid=(m // tm,),
            in_specs=in_specs + [pl.BlockSpec((1, d), lambda i: (0, 0))],
            out_specs=row,
            compiler_params=params,
            name="merge_last",
        )(*args, *nxt)
    norm_g, scale, shift = nxt
    return pl.pallas_call(
        _merge_kernel,
        out_shape=(jax.ShapeDtypeStruct((m, d), F32), jax.ShapeDtypeStruct((m, d), BF16)),
        grid=(m // tm,),
        in_specs=in_specs + [pl.BlockSpec((None, 1, d), lambda i: (layer + 1, 0, 0)), per_batch, per_batch],
        out_specs=(row, row),
        compiler_params=params,
        name="merge_out",
    )(*args, norm_g, scale, shift)


def kernel(x, c, norm_g, w_ada, b_ada, w_in, pool_w, pool_scale, attn_sink, conv_dw, conv_dw_b,
           conv_ln_g, conv_ln_b, conv_pw, w_branch_pool, w_branch_attn, w_branch_conv, w_out,
           final_g):
    batch, seq, d = x.shape
    depth = w_in.shape[0]
    m = batch * seq
    assert d == D_MODEL and w_in.shape[-1] == IN_WIDTH and batch <= 8
    assert seq % BRANCH_TM == 0 and seq % MERGE_TM == 0

    w_in_b = w_in.astype(BF16)
    pool_w_b = pool_w.astype(BF16)
    conv_pw_b = conv_pw.astype(BF16)
    w_bp = w_branch_pool.astype(BF16)
    w_ba = w_branch_attn.astype(BF16)
    w_bc = w_branch_conv.astype(BF16)
    w_o = w_out.astype(BF16)

    c_pad = jnp.zeros((8, d), F32).at[:batch].set(c)
    mod = _ada_mod(c_pad, w_ada, b_ada)
    shifts = [mod[l, :batch, 0:d].reshape(batch, 1, d) for l in range(depth)]
    scales = [mod[l, :batch, d:2 * d].reshape(batch, 1, d) for l in range(depth)]
    gates = [mod[l, :batch, 2 * d:3 * d].reshape(batch, 1, d) for l in range(depth)]

    norm_g3 = norm_g.reshape(depth, 1, d)
    pool_scale3 = pool_scale.reshape(depth, 1, POOL_WIDTH)
    dwb3 = conv_dw_b.reshape(depth, 1, CONV_WIDTH)
    lng3 = conv_ln_g.reshape(depth, 1, CONV_WIDTH)
    lnb3 = conv_ln_b.reshape(depth, 1, CONV_WIDTH)

    xf = x.reshape(m, d)
    h = _first_norm(xf, norm_g3, scales[0], shifts[0], seq)
    for layer in range(depth):
        y_pool, g_pool = _pool_branch(h, w_in_b, pool_w_b, pool_scale3, layer, seq)
        y_attn, g_attn = _attn_branch(h, w_in_b, attn_sink, layer, seq)
        y_conv, g_conv = _conv_branch(h, w_in_b, conv_dw, dwb3, lng3, lnb3, conv_pw_b, layer, seq)
        if layer + 1 < depth:
            nxt = (norm_g3, scales[layer + 1], shifts[layer + 1])
        else:
            nxt = (final_g.reshape(1, d),)
        res = _merge_out(xf, (y_pool, y_attn, y_conv), (g_pool, g_attn, g_conv), w_bp, w_ba, w_bc, w_o,
                         gates[layer], layer, seq, nxt)
        if layer + 1 < depth:
            xf, h = res
        else:
            xf = res
    return xf.reshape(batch, seq, d)
```

```python
import functools

import jax
import jax.numpy as jnp
from jax import lax
from jax.experimental import pallas as pl
from jax.experimental.pallas import tpu as pltpu

D_MODEL = 2048
CHUNK = 64
EPS = 1e-6
POOL_WIDTH = 1024
POOL_WINDOWS = (2, 4, 8, 16)
N_POOL_GROUPS = 4
POOL_GROUP = POOL_WIDTH // N_POOL_GROUPS
N_Q_HEADS = 16
N_KV_HEADS = 4
HEAD_DIM = 64
GROUP = N_Q_HEADS // N_KV_HEADS
ATTN_WIDTH = N_Q_HEADS * HEAD_DIM
KV_WIDTH = N_KV_HEADS * HEAD_DIM
WINDOW_CHUNKS = 2
N_KEYS = (WINDOW_CHUNKS + 1) * CHUNK
CONV_WIDTH = 1024
CONV_KERNEL = 31
IN_WIDTH = 2 * POOL_WIDTH + 2 * ATTN_WIDTH + 2 * KV_WIDTH + 3 * CONV_WIDTH + 3 * D_MODEL

POOL_COL = 0
Q_COL = 2 * POOL_WIDTH
KV_COL = Q_COL + ATTN_WIDTH
ATTN_Z_COL = KV_COL + 2 * KV_WIDTH
CONV_COL = ATTN_Z_COL + ATTN_WIDTH
GATE_COL = CONV_COL + 3 * CONV_WIDTH
GATE_BLOCK = 512
CONV_BLOCK = 1536
CONV_CHAIN = 256

POOL_HALO = 16
CONV_HALO = 32
ATTN_HALO = WINDOW_CHUNKS * CHUNK
SUBLANES = 8

BRANCH_TM = 512
MERGE_TM = 256
VMEM_LIMIT = 56 * 1024 * 1024

F32 = jnp.float32
BF16 = jnp.bfloat16


def _sigmoid(v):
    return jax.nn.sigmoid(v)


def _silu(v):
    return v * jax.nn.sigmoid(v)


def _dot(a, b):
    return jnp.dot(a, b, preferred_element_type=F32)


def _ada_norm(x, g, scale, shift):
    ms = jnp.mean(x * x, axis=-1, keepdims=True)
    return x * lax.rsqrt(ms + EPS) * g * (1.0 + scale) + shift


def _ada_kernel(c_ref, w_ref, b_ref, o_ref):
    ca = _silu(c_ref[...])
    o_ref[...] = jnp.dot(ca, w_ref[...], preferred_element_type=F32,
                         precision=lax.Precision.HIGHEST) + b_ref[...]


def _ada_mod(c_pad, w_ada, b_ada):
    depth, d, n = w_ada.shape
    tn = 2048
    return pl.pallas_call(
        _ada_kernel,
        out_shape=jax.ShapeDtypeStruct((depth, c_pad.shape[0], n), F32),
        grid=(depth, n // tn),
        in_specs=[
            pl.BlockSpec(c_pad.shape, lambda l, j: (0, 0)),
            pl.BlockSpec((None, d, tn), lambda l, j: (l, 0, j)),
            pl.BlockSpec((None, 1, tn), lambda l, j: (l, 0, j)),
        ],
        out_specs=pl.BlockSpec((None, c_pad.shape[0], tn), lambda l, j: (l, 0, j)),
        compiler_params=pltpu.CompilerParams(dimension_semantics=("parallel", "parallel")),
        name="ada_mod",
    )(c_pad, w_ada, b_ada.reshape(depth, 1, n))


def _norm_kernel(x_ref, g_ref, sc_ref, sh_ref, h_ref):
    h_ref[...] = _ada_norm(x_ref[...], g_ref[...], sc_ref[...], sh_ref[...]).astype(BF16)


def _first_norm(xf, norm_g, scale, shift, seq):
    m, d = xf.shape
    tm = 512
    vec = lambda i: ((i * tm) // seq, 0, 0)
    return pl.pallas_call(
        _norm_kernel,
        out_shape=jax.ShapeDtypeStruct((m, d), BF16),
        grid=(m // tm,),
        in_specs=[pl.BlockSpec((tm, d), lambda i: (i, 0)),
                  pl.BlockSpec((None, 1, d), lambda i: (0, 0, 0)),
                  pl.BlockSpec((None, 1, d), vec),
                  pl.BlockSpec((None, 1, d), vec)],
        out_specs=pl.BlockSpec((tm, d), lambda i: (i, 0)),
        compiler_params=pltpu.CompilerParams(dimension_semantics=("parallel",)),
        name="first_norm",
    )(xf, norm_g, scale, shift)


def _gate_step(h_ref, wg_ref, gate_ref, k):
    gate_ref[:, k * GATE_BLOCK:(k + 1) * GATE_BLOCK] = _sigmoid(_dot(h_ref[...], wg_ref[...])).astype(BF16)


def _pool_branch_kernel(h_ref, wp_ref, wg0, wg1, wg2, wg3, pw_ref, ps_ref, y_ref, gate_ref, carry_ref,
                        *, tiles_per_seq):
    tm = h_ref.shape[0]
    ti = pl.program_id(0) % tiles_per_seq

    @pl.when(pl.program_id(0) == 0)
    def _():
        carry_ref[...] = jnp.zeros(carry_ref.shape, F32)

    t = ti * tm + lax.broadcasted_iota(jnp.int32, (tm, POOL_GROUP), 0)
    for gi, w in enumerate(POOL_WINDOWS):
        cols = slice(gi * POOL_GROUP, (gi + 1) * POOL_GROUP)
        zcols = slice(POOL_WIDTH + gi * POOL_GROUP, POOL_WIDTH + (gi + 1) * POOL_GROUP)
        ug = _dot(h_ref[...], wp_ref[:, cols])
        zg = _dot(h_ref[...], wp_ref[:, zcols])
        halo = jnp.where(ti == 0, 0.0, carry_ref[:, cols])
        carry_ref[:, cols] = ug[tm - POOL_HALO:, :]
        s = jnp.concatenate([halo, ug], axis=0)
        k = 1
        while k < w:
            s = s[k:] + s[:-k]
            k *= 2
        off = POOL_HALO - (w - 1)
        s = s[off:off + tm]
        cnt = jnp.minimum(t + 1, w).astype(F32)
        mixed = (s / cnt - ug).astype(BF16)
        y = _dot(mixed, pw_ref[gi]) * ps_ref[:, cols] * _silu(zg)
        y_ref[:, cols] = y.astype(BF16)
        _gate_step(h_ref, (wg0, wg1, wg2, wg3)[gi], gate_ref, gi)


def _attn_branch_kernel(sink_ref, h_ref, wq_ref, wkv_ref, wz0_ref, wz1_ref, wg0, wg1, wg2, wg3,
                        y_ref, gate_ref, kvx_ref, *, tiles_per_seq, layer):
    tm = h_ref.shape[0]
    ti = pl.program_id(0) % tiles_per_seq
    head_cols = GROUP * HEAD_DIM

    @pl.when(pl.program_id(0) == 0)
    def _():
        kvx_ref[0:ATTN_HALO, :] = jnp.zeros((ATTN_HALO, 2 * KV_WIDTH), BF16)

    kvx_ref[ATTN_HALO:, :] = _dot(h_ref[...], wkv_ref[...]).astype(BF16)
    row_head = lax.broadcasted_iota(jnp.int32, (GROUP * CHUNK, 1), 0) // CHUNK
    key_chunk = lax.broadcasted_iota(jnp.int32, (1, N_KEYS), 1) // CHUNK
    for kvh in range(N_KV_HEADS):
        cols = slice(kvh * head_cols, (kvh + 1) * head_cols)
        q = _dot(h_ref[...], wq_ref[:, cols]).astype(BF16)
        wz_ref = wz0_ref if kvh < N_KV_HEADS // 2 else wz1_ref
        zoff = (kvh % (N_KV_HEADS // 2)) * head_cols
        sz = _silu(_dot(h_ref[...], wz_ref[:, zoff:zoff + head_cols]))
        heads = [kvh * GROUP + g for g in range(GROUP)]
        sk = jnp.full((GROUP * CHUNK, 1), sink_ref[layer, heads[GROUP - 1]], F32)
        for g in range(GROUP - 2, -1, -1):
            sk = jnp.where(row_head == g, sink_ref[layer, heads[g]], sk)
        for c in range(tm // CHUNK):
            r0 = c * CHUNK
            qc = q[r0:r0 + CHUNK, :]
            qs = jnp.concatenate([qc[:, g * HEAD_DIM:(g + 1) * HEAD_DIM] for g in range(GROUP)], axis=0)
            kk = kvx_ref[r0:r0 + N_KEYS, kvh * HEAD_DIM:(kvh + 1) * HEAD_DIM]
            vv = kvx_ref[r0:r0 + N_KEYS, KV_WIDTH + kvh * HEAD_DIM:KV_WIDTH + (kvh + 1) * HEAD_DIM]
            n = ti * (tm // CHUNK) + c
            bias = jnp.where(n - WINDOW_CHUNKS + key_chunk >= 0, 0.0, -jnp.inf).astype(F32)
            s = lax.dot_general(qs, kk, (((1,), (1,)), ((), ())), preferred_element_type=F32)
            s = s * (HEAD_DIM ** -0.5) + bias
            mx = jnp.maximum(jnp.max(s, axis=-1, keepdims=True), sk)
            p = jnp.exp(s - mx)
            denom = jnp.sum(p, axis=-1, keepdims=True) + jnp.exp(sk - mx)
            o = _dot(p.astype(BF16), vv) / denom
            oc = jnp.concatenate([o[g * CHUNK:(g + 1) * CHUNK] for g in range(GROUP)], axis=1)
            y_ref[r0:r0 + CHUNK, cols] = (oc * sz[r0:r0 + CHUNK, :]).astype(BF16)
        _gate_step(h_ref, (wg0, wg1, wg2, wg3)[kvh], gate_ref, kvh)
    kvx_ref[0:ATTN_HALO, :] = kvx_ref[tm:tm + ATTN_HALO, :]


def _conv_branch_kernel(h_ref, w0_ref, w1_ref, wg0, wg1, wg2, wg3, dw_ref, dwb_ref, lg_ref, lb_ref,
                        pw_ref, y_ref, gate_ref, gs_ref, carry_ref, yscr_ref, zs_ref,
                        *, tiles_per_seq, rows, cb, tail_rows):
    tm = h_ref.shape[0]
    ext = tm + CONV_HALO
    base = CONV_HALO - (CONV_KERNEL - 1)
    ti = pl.program_id(0) % tiles_per_seq

    @pl.when(pl.program_id(0) == 0)
    def _():
        carry_ref[...] = jnp.zeros(carry_ref.shape, F32)

    def wcols(c0):
        if c0 < CONV_BLOCK:
            return w0_ref[:, c0:c0 + cb]
        return w1_ref[:, c0 - CONV_BLOCK:c0 - CONV_BLOCK + cb]

    for k in range(CONV_WIDTH // cb):
        cols = slice(k * cb, (k + 1) * cb)
        gk = gs_ref.at[k % 2]
        a = _dot(h_ref[...], wcols(k * cb))
        b = _dot(h_ref[...], wcols(CONV_WIDTH + k * cb))
        z = _dot(h_ref[...], wcols(2 * CONV_WIDTH + k * cb))
        g = a * _sigmoid(b)
        gk[0, 0:CONV_HALO, :] = jnp.where(ti == 0, 0.0, carry_ref[:, cols])
        gk[0, CONV_HALO:, :] = g
        carry_ref[:, cols] = g[tm - CONV_HALO:, :]
        zs_ref[:, cols] = _silu(z)
        for s in range(1, SUBLANES):
            gk[s, 0:ext - SUBLANES, :] = gk[0, s:s + ext - SUBLANES, :]
        for rc in range(tm // rows):
            acc = jnp.zeros((rows, cb), F32)
            for j in range(CONV_KERNEL):
                q, s = divmod(base + j, SUBLANES)
                r0 = rc * rows + q * SUBLANES
                acc = acc + dw_ref[j:j + 1, cols] * gk[s, r0:r0 + rows, :]
            yscr_ref[rc * rows:(rc + 1) * rows, cols] = acc + dwb_ref[:, cols]
        _gate_step(h_ref, (wg0, wg1, wg2, wg3)[k], gate_ref, k)
    for rb in range(tm // tail_rows):
        rs = slice(rb * tail_rows, (rb + 1) * tail_rows)
        y = yscr_ref[rs, :]
        mu = jnp.mean(y, axis=-1, keepdims=True)
        yc = y - mu
        var = jnp.mean(yc * yc, axis=-1, keepdims=True)
        yn = yc * lax.rsqrt(var + EPS) * lg_ref[...] + lb_ref[...]
        out = _dot(_silu(yn).astype(BF16), pw_ref[...])
        y_ref[rs, :] = (out * zs_ref[rs, :]).astype(BF16)


def _branch_specs(m, d, layer, gate_index):
    tm = BRANCH_TM
    once = pl.Buffered(1)

    def w_spec(width, col):
        assert col % width == 0
        return pl.BlockSpec((None, d, width), lambda i: (layer, 0, col // width), pipeline_mode=once)

    def row_spec(width):
        return pl.BlockSpec((tm, width), lambda i: (i, 0))

    gate_col = GATE_COL + gate_index * D_MODEL
    gate_w_specs = [w_spec(GATE_BLOCK, gate_col + k * GATE_BLOCK) for k in range(D_MODEL // GATE_BLOCK)]
    out_shape = (jax.ShapeDtypeStruct((m, POOL_WIDTH), BF16), jax.ShapeDtypeStruct((m, d), BF16))
    params = pltpu.CompilerParams(dimension_semantics=("arbitrary",), vmem_limit_bytes=VMEM_LIMIT)
    return tm, once, w_spec, row_spec, gate_w_specs, out_shape, params


def _pool_branch(h, w_in, pool_w, pool_scale, layer, seq):
    m, d = h.shape
    tm, once, w_spec, row_spec, gate_w_specs, out_shape, params = _branch_specs(m, d, layer, 0)
    return pl.pallas_call(
        functools.partial(_pool_branch_kernel, tiles_per_seq=seq // tm),
        out_shape=out_shape,
        grid=(m // tm,),
        in_specs=[row_spec(d), w_spec(2 * POOL_WIDTH, POOL_COL)] + gate_w_specs + [
            pl.BlockSpec((None, N_POOL_GROUPS, POOL_GROUP, POOL_GROUP), lambda i: (layer, 0, 0, 0),
                         pipeline_mode=once),
            pl.BlockSpec((None, 1, POOL_WIDTH), lambda i: (layer, 0, 0)),
        ],
        out_specs=(row_spec(POOL_WIDTH), row_spec(d)),
        scratch_shapes=[pltpu.VMEM((POOL_HALO, POOL_WIDTH), F32)],
        compiler_params=params,
        name="pool_branch",
    )(h, w_in, w_in, w_in, w_in, w_in, pool_w, pool_scale)


def _attn_branch(h, w_in, attn_sink, layer, seq):
    m, d = h.shape
    tm, once, w_spec, row_spec, gate_w_specs, out_shape, params = _branch_specs(m, d, layer, 1)
    half = ATTN_WIDTH // 2
    return pl.pallas_call(
        functools.partial(_attn_branch_kernel, tiles_per_seq=seq // tm, layer=layer),
        out_shape=out_shape,
        grid=(m // tm,),
        in_specs=[pl.BlockSpec(memory_space=pltpu.SMEM), row_spec(d),
                  w_spec(ATTN_WIDTH, Q_COL), w_spec(2 * KV_WIDTH, KV_COL),
                  w_spec(half, ATTN_Z_COL), w_spec(half, ATTN_Z_COL + half)] + gate_w_specs,
        out_specs=(row_spec(ATTN_WIDTH), row_spec(d)),
        scratch_shapes=[pltpu.VMEM((tm + ATTN_HALO, 2 * KV_WIDTH), BF16)],
        compiler_params=params,
        name="attn_branch",
    )(attn_sink, h, w_in, w_in, w_in, w_in, w_in, w_in, w_in, w_in)


def _conv_branch(h, w_in, conv_dw, conv_dw_b, conv_ln_g, conv_ln_b, conv_pw, layer, seq):
    m, d = h.shape
    tm, once, w_spec, row_spec, gate_w_specs, out_shape, params = _branch_specs(m, d, layer, 2)
    vec = pl.BlockSpec((None, 1, CONV_WIDTH), lambda i: (layer, 0, 0))
    return pl.pallas_call(
        functools.partial(_conv_branch_kernel, tiles_per_seq=seq // tm, rows=64, cb=CONV_CHAIN,
                          tail_rows=256),
        out_shape=out_shape,
        grid=(m // tm,),
        in_specs=[row_spec(d), w_spec(CONV_BLOCK, CONV_COL), w_spec(CONV_BLOCK, CONV_COL + CONV_BLOCK)]
        + gate_w_specs + [
            pl.BlockSpec((None, CONV_KERNEL, CONV_WIDTH), lambda i: (layer, 0, 0)),
            vec, vec, vec,
            pl.BlockSpec((None, CONV_WIDTH, CONV_WIDTH), lambda i: (layer, 0, 0), pipeline_mode=once),
        ],
        out_specs=(row_spec(CONV_WIDTH), row_spec(d)),
        scratch_shapes=[pltpu.VMEM((2, SUBLANES, tm + CONV_HALO, CONV_CHAIN), F32),
                        pltpu.VMEM((CONV_HALO, CONV_WIDTH), F32),
                        pltpu.VMEM((tm, CONV_WIDTH), F32),
                        pltpu.VMEM((tm, CONV_WIDTH), F32)],
        compiler_params=params,
        name="conv_branch",
    )(h, w_in, w_in, w_in, w_in, w_in, w_in, conv_dw, conv_dw_b, conv_ln_g, conv_ln_b, conv_pw)


def _merged_update(x_ref, yp_ref, ya_ref, yc_ref, gp_ref, ga_ref, gc_ref, wp_ref, wa_ref, wc_ref,
                   wo_ref, gate_ref):
    def branch(y_ref, g_ref, w_ref):
        return g_ref[...].astype(F32) * _dot(y_ref[...], w_ref[...])

    merged = branch(yp_ref, gp_ref, wp_ref) + branch(ya_ref, ga_ref, wa_ref) + branch(yc_ref, gc_ref, wc_ref)
    return x_ref[...] + gate_ref[...] * _dot(merged.astype(BF16), wo_ref[...])


def _merge_kernel(x_ref, yp_ref, ya_ref, yc_ref, gp_ref, ga_ref, gc_ref, wp_ref, wa_ref, wc_ref,
                  wo_ref, gate_ref, ng_ref, nsc_ref, nsh_ref, o_ref, h_ref):
    xn = _merged_update(x_ref, yp_ref, ya_ref, yc_ref, gp_ref, ga_ref, gc_ref, wp_ref, wa_ref, wc_ref,
                        wo_ref, gate_ref)
    o_ref[...] = xn
    h_ref[...] = _ada_norm(xn, ng_ref[...], nsc_ref[...], nsh_ref[...]).astype(BF16)


def _merge_last_kernel(x_ref, yp_ref, ya_ref, yc_ref, gp_ref, ga_ref, gc_ref, wp_ref, wa_ref, wc_ref,
                       wo_ref, gate_ref, fg_ref, o_ref):
    xn = _merged_update(x_ref, yp_ref, ya_ref, yc_ref, gp_ref, ga_ref, gc_ref, wp_ref, wa_ref, wc_ref,
                        wo_ref, gate_ref)
    ms = jnp.mean(xn * xn, axis=-1, keepdims=True)
    o_ref[...] = xn * lax.rsqrt(ms + EPS) * fg_ref[...]


def _merge_out(xf, ys, gates, w_bp, w_ba, w_bc, w_out, gate, layer, seq, nxt):
    m, d = xf.shape
    tm = MERGE_TM
    once = pl.Buffered(1)
    row = pl.BlockSpec((tm, d), lambda i: (i, 0))
    ybs = pl.BlockSpec((tm, POOL_WIDTH), lambda i: (i, 0))
    wbs = pl.BlockSpec((None, POOL_WIDTH, d), lambda i: (layer, 0, 0), pipeline_mode=once)
    per_batch = pl.BlockSpec((None, 1, d), lambda i: ((i * tm) // seq, 0, 0))
    in_specs = [row, ybs, ybs, ybs, row, row, row, wbs, wbs, wbs,
                pl.BlockSpec((None, d, d), lambda i: (layer, 0, 0), pipeline_mode=once), per_batch]
    args = [xf, *ys, *gates, w_bp, w_ba, w_bc, w_out, gate]
    params = pltpu.CompilerParams(dimension_semantics=("parallel",), vmem_limit_bytes=VMEM_LIMIT)
    if len(nxt) == 1:
        return pl.pallas_call(
            _merge_last_kernel,
            out_shape=jax.ShapeDtypeStruct((m, d), F32),
            grid=(m // tm,),
            in_specs=in_specs + [pl.BlockSpec((1, d), lambda i: (0, 0))],
            out_specs=row,
            compiler_params=params,
            name="merge_last",
        )(*args, *nxt)
    norm_g, scale, shift = nxt
    return pl.pallas_call(
        _merge_kernel,
        out_shape=(jax.ShapeDtypeStruct((m, d), F32), jax.ShapeDtypeStruct((m, d), BF16)),
        grid=(m // tm,),
        in_specs=in_specs + [pl.BlockSpec((None, 1, d), lambda i: (layer + 1, 0, 0)), per_batch, per_batch],
        out_specs=(row, row),
        compiler_params=params,
        name="merge_out",
    )(*args, norm_g, scale, shift)


def kernel(x, c, norm_g, w_ada, b_ada, w_in, pool_w, pool_scale, attn_sink, conv_dw, conv_dw_b,
           conv_ln_g, conv_ln_b, conv_pw, w_branch_pool, w_branch_attn, w_branch_conv, w_out,
           final_g):
    batch, seq, d = x.shape
    depth = w_in.shape[0]
    m = batch * seq
    assert d == D_MODEL and w_in.shape[-1] == IN_WIDTH and batch <= 8
    assert seq % BRANCH_TM == 0 and seq % MERGE_TM == 0

    w_in_b = w_in.astype(BF16)
    pool_w_b = pool_w.astype(BF16)
    conv_pw_b = conv_pw.astype(BF16)
    w_bp = w_branch_pool.astype(BF16)
    w_ba = w_branch_attn.astype(BF16)
    w_bc = w_branch_conv.astype(BF16)
    w_o = w_out.astype(BF16)

    c_pad = jnp.zeros((8, d), F32).at[:batch].set(c)
    mod = _ada_mod(c_pad, w_ada, b_ada)
    shifts = [mod[l, :batch, 0:d].reshape(batch, 1, d) for l in range(depth)]
    scales = [mod[l, :batch, d:2 * d].reshape(batch, 1, d) for l in range(depth)]
    gates = [mod[l, :batch, 2 * d:3 * d].reshape(batch, 1, d) for l in range(depth)]

    norm_g3 = norm_g.reshape(depth, 1, d)
    pool_scale3 = pool_scale.reshape(depth, 1, POOL_WIDTH)
    dwb3 = conv_dw_b.reshape(depth, 1, CONV_WIDTH)
    lng3 = conv_ln_g.reshape(depth, 1, CONV_WIDTH)
    lnb3 = conv_ln_b.reshape(depth, 1, CONV_WIDTH)

    xf = x.reshape(m, d)
    h = _first_norm(xf, norm_g3, scales[0], shifts[0], seq)
    for layer in range(depth):
        y_pool, g_pool = _pool_branch(h, w_in_b, pool_w_b, pool_scale3, layer, seq)
        y_attn, g_attn = _attn_branch(h, w_in_b, attn_sink, layer, seq)
        y_conv, g_conv = _conv_branch(h, w_in_b, conv_dw, dwb3, lng3, lnb3, conv_pw_b, layer, seq)
        if layer + 1 < depth:
            nxt = (norm_g3, scales[layer + 1], shifts[layer + 1])
        else:
            nxt = (final_g.reshape(1, d),)
        res = _merge_out(xf, (y_pool, y_attn, y_conv), (g_pool, g_attn, g_conv), w_bp, w_ba, w_bc, w_o,
                         gates[layer], layer, seq, nxt)
        if layer + 1 < depth:
            xf, h = res
        else:
            xf = res
    return xf.reshape(batch, seq, d)
```

```python
import functools

import jax
import jax.numpy as jnp
from jax import lax
from jax.experimental import pallas as pl
from jax.experimental.pallas import tpu as pltpu

D_MODEL = 2048
CHUNK = 64
EPS = 1e-6
POOL_WIDTH = 1024
POOL_WINDOWS = (2, 4, 8, 16)
N_POOL_GROUPS = 4
POOL_GROUP = POOL_WIDTH // N_POOL_GROUPS
N_Q_HEADS = 16
N_KV_HEADS = 4
HEAD_DIM = 64
GROUP = N_Q_HEADS // N_KV_HEADS
ATTN_WIDTH = N_Q_HEADS * HEAD_DIM
KV_WIDTH = N_KV_HEADS * HEAD_DIM
WINDOW_CHUNKS = 2
N_KEYS = (WINDOW_CHUNKS + 1) * CHUNK
CONV_WIDTH = 1024
CONV_KERNEL = 31
IN_WIDTH = 2 * POOL_WIDTH + 2 * ATTN_WIDTH + 2 * KV_WIDTH + 3 * CONV_WIDTH + 3 * D_MODEL

POOL_COL = 0
Q_COL = 2 * POOL_WIDTH
KV_COL = Q_COL + ATTN_WIDTH
ATTN_Z_COL = KV_COL + 2 * KV_WIDTH
CONV_COL = ATTN_Z_COL + ATTN_WIDTH
GATE_COL = CONV_COL + 3 * CONV_WIDTH
GATE_BLOCK = 512
CONV_BLOCK = 1536
CONV_CHAIN = 256
CAST_BLOCK = 512

POOL_HALO = 16
CONV_HALO = 32
ATTN_HALO = WINDOW_CHUNKS * CHUNK
SUBLANES = 8

BRANCH_TM = 512
MERGE_TM = 256
VMEM_LIMIT = 56 * 1024 * 1024

F32 = jnp.float32
BF16 = jnp.bfloat16


def _sigmoid(v):
    return jax.nn.sigmoid(v)


def _silu(v):
    return v * jax.nn.sigmoid(v)


def _dot(a, b):
    return jnp.dot(a, b, preferred_element_type=F32)


def _ada_norm(x, g, scale, shift):
    ms = jnp.mean(x * x, axis=-1, keepdims=True)
    return x * lax.rsqrt(ms + EPS) * g * (1.0 + scale) + shift


def _ada_kernel(c_ref, w_ref, b_ref, o_ref):
    ca = _silu(c_ref[...])
    o_ref[...] = jnp.dot(ca, w_ref[...], preferred_element_type=F32,
                         precision=lax.Precision.HIGHEST) + b_ref[...]


def _ada_mod(c_pad, w_ada, b_ada):
    depth, d, n = w_ada.shape
    tn = 2048
    return pl.pallas_call(
        _ada_kernel,
        out_shape=jax.ShapeDtypeStruct((depth, c_pad.shape[0], n), F32),
        grid=(depth, n // tn),
        in_specs=[
            pl.BlockSpec(c_pad.shape, lambda l, j: (0, 0)),
            pl.BlockSpec((None, d, tn), lambda l, j: (l, 0, j)),
            pl.BlockSpec((None, 1, tn), lambda l, j: (l, 0, j)),
        ],
        out_specs=pl.BlockSpec((None, c_pad.shape[0], tn), lambda l, j: (l, 0, j)),
        compiler_params=pltpu.CompilerParams(dimension_semantics=("parallel", "parallel")),
        name="ada_mod",
    )(c_pad, w_ada, b_ada.reshape(depth, 1, n))


def _norm_kernel(x_ref, g_ref, sc_ref, sh_ref, h_ref):
    h_ref[...] = _ada_norm(x_ref[...], g_ref[...], sc_ref[...], sh_ref[...]).astype(BF16)


def _first_norm(xf, norm_g, scale, shift, seq):
    m, d = xf.shape
    tm = 512
    vec = lambda i: ((i * tm) // seq, 0, 0)
    return pl.pallas_call(
        _norm_kernel,
        out_shape=jax.ShapeDtypeStruct((m, d), BF16),
        grid=(m // tm,),
        in_specs=[pl.BlockSpec((tm, d), lambda i: (i, 0)),
                  pl.BlockSpec((None, 1, d), lambda i: (0, 0, 0)),
                  pl.BlockSpec((None, 1, d), vec),
                  pl.BlockSpec((None, 1, d), vec)],
        out_specs=pl.BlockSpec((tm, d), lambda i: (i, 0)),
        compiler_params=pltpu.CompilerParams(dimension_semantics=("parallel",)),
        name="first_norm",
    )(xf, norm_g, scale, shift)


def _gate_step(h_ref, wg_ref, gate_ref, k):
    gate_ref[:, k * GATE_BLOCK:(k + 1) * GATE_BLOCK] = _sigmoid(_dot(h_ref[...], wg_ref[...])).astype(BF16)


def _pool_branch_kernel(h_ref, wp_ref, wg0, wg1, wg2, wg3, pw_ref, ps_ref, y_ref, gate_ref, carry_ref,
                        *, tiles_per_seq):
    tm = h_ref.shape[0]
    ti = pl.program_id(0) % tiles_per_seq

    @pl.when(pl.program_id(0) == 0)
    def _():
        carry_ref[...] = jnp.zeros(carry_ref.shape, F32)

    t = ti * tm + lax.broadcasted_iota(jnp.int32, (tm, POOL_GROUP), 0)
    for gi, w in enumerate(POOL_WINDOWS):
        cols = slice(gi * POOL_GROUP, (gi + 1) * POOL_GROUP)
        zcols = slice(POOL_WIDTH + gi * POOL_GROUP, POOL_WIDTH + (gi + 1) * POOL_GROUP)
        ug = _dot(h_ref[...], wp_ref[:, cols])
        zg = _dot(h_ref[...], wp_ref[:, zcols])
        halo = jnp.where(ti == 0, 0.0, carry_ref[:, cols])
        carry_ref[:, cols] = ug[tm - POOL_HALO:, :]
        s = jnp.concatenate([halo, ug], axis=0)
        k = 1
        while k < w:
            s = s[k:] + s[:-k]
            k *= 2
        off = POOL_HALO - (w - 1)
        s = s[off:off + tm]
        cnt = jnp.minimum(t + 1, w).astype(F32)
        mixed = (s / cnt - ug).astype(BF16)
        y = _dot(mixed, pw_ref[gi]) * ps_ref[:, cols] * _silu(zg)
        y_ref[:, cols] = y.astype(BF16)
        _gate_step(h_ref, (wg0, wg1, wg2, wg3)[gi], gate_ref, gi)


def _attn_branch_kernel(sink_ref, h_ref, wq_ref, wkv_ref, wz0_ref, wz1_ref, wg0, wg1, wg2, wg3,
                        y_ref, gate_ref, kvx_ref, *, tiles_per_seq, layer):
    tm = h_ref.shape[0]
    ti = pl.program_id(0) % tiles_per_seq
    head_cols = GROUP * HEAD_DIM

    @pl.when(pl.program_id(0) == 0)
    def _():
        kvx_ref[0:ATTN_HALO, :] = jnp.zeros((ATTN_HALO, 2 * KV_WIDTH), BF16)

    kvx_ref[ATTN_HALO:, :] = _dot(h_ref[...], wkv_ref[...]).astype(BF16)
    row_head = lax.broadcasted_iota(jnp.int32, (GROUP * CHUNK, 1), 0) // CHUNK
    key_chunk = lax.broadcasted_iota(jnp.int32, (1, N_KEYS), 1) // CHUNK
    for kvh in range(N_KV_HEADS):
        cols = slice(kvh * head_cols, (kvh + 1) * head_cols)
        q = _dot(h_ref[...], wq_ref[:, cols]).astype(BF16)
        wz_ref = wz0_ref if kvh < N_KV_HEADS // 2 else wz1_ref
        zoff = (kvh % (N_KV_HEADS // 2)) * head_cols
        sz = _silu(_dot(h_ref[...], wz_ref[:, zoff:zoff + head_cols]))
        heads = [kvh * GROUP + g for g in range(GROUP)]
        sk = jnp.full((GROUP * CHUNK, 1), sink_ref[layer, heads[GROUP - 1]], F32)
        for g in range(GROUP - 2, -1, -1):
            sk = jnp.where(row_head == g, sink_ref[layer, heads[g]], sk)
        for c in range(tm // CHUNK):
            r0 = c * CHUNK
            qc = q[r0:r0 + CHUNK, :]
            qs = jnp.concatenate([qc[:, g * HEAD_DIM:(g + 1) * HEAD_DIM] for g in range(GROUP)], axis=0)
            kk = kvx_ref[r0:r0 + N_KEYS, kvh * HEAD_DIM:(kvh + 1) * HEAD_DIM]
            vv = kvx_ref[r0:r0 + N_KEYS, KV_WIDTH + kvh * HEAD_DIM:KV_WIDTH + (kvh + 1) * HEAD_DIM]
            n = ti * (tm // CHUNK) + c
            bias = jnp.where(n - WINDOW_CHUNKS + key_chunk >= 0, 0.0, -jnp.inf).astype(F32)
            s = lax.dot_general(qs, kk, (((1,), (1,)), ((), ())), preferred_element_type=F32)
            s = s * (HEAD_DIM ** -0.5) + bias
            mx = jnp.maximum(jnp.max(s, axis=-1, keepdims=True), sk)
            p = jnp.exp(s - mx)
            denom = jnp.sum(p, axis=-1, keepdims=True) + jnp.exp(sk - mx)
            o = _dot(p.astype(BF16), vv) / denom
            oc = jnp.concatenate([o[g * CHUNK:(g + 1) * CHUNK] for g in range(GROUP)], axis=1)
            y_ref[r0:r0 + CHUNK, cols] = (oc * sz[r0:r0 + CHUNK, :]).astype(BF16)
        _gate_step(h_ref, (wg0, wg1, wg2, wg3)[kvh], gate_ref, kvh)
    kvx_ref[0:ATTN_HALO, :] = kvx_ref[tm:tm + ATTN_HALO, :]


def _conv_branch_kernel(h_ref, w0_ref, w1_ref, wg0, wg1, wg2, wg3, dw_ref, dwb_ref, lg_ref, lb_ref,
                        pw_ref, y_ref, gate_ref, gs_ref, carry_ref, yscr_ref, zs_ref,
                        *, tiles_per_seq, rows, cb, tail_rows):
    tm = h_ref.shape[0]
    ext = tm + CONV_HALO
    base = CONV_HALO - (CONV_KERNEL - 1)
    ti = pl.program_id(0) % tiles_per_seq

    @pl.when(pl.program_id(0) == 0)
    def _():
        carry_ref[...] = jnp.zeros(carry_ref.shape, F32)

    def wcols(c0):
        if c0 < CONV_BLOCK:
            return w0_ref[:, c0:c0 + cb]
        return w1_ref[:, c0 - CONV_BLOCK:c0 - CONV_BLOCK + cb]

    for k in range(CONV_WIDTH // cb):
        cols = slice(k * cb, (k + 1) * cb)
        gk = gs_ref.at[k % 2]
        a = _dot(h_ref[...], wcols(k * cb))
        b = _dot(h_ref[...], wcols(CONV_WIDTH + k * cb))
        z = _dot(h_ref[...], wcols(2 * CONV_WIDTH + k * cb))
        g = a * _sigmoid(b)
        gk[0, 0:CONV_HALO, :] = jnp.where(ti == 0, 0.0, carry_ref[:, cols])
        gk[0, CONV_HALO:, :] = g
        carry_ref[:, cols] = g[tm - CONV_HALO:, :]
        zs_ref[:, cols] = _silu(z)
        for s in range(1, SUBLANES):
            gk[s, 0:ext - SUBLANES, :] = gk[0, s:s + ext - SUBLANES, :]
        for rc in range(tm // rows):
            acc = jnp.zeros((rows, cb), F32)
            for j in range(CONV_KERNEL):
                q, s = divmod(base + j, SUBLANES)
                r0 = rc * rows + q * SUBLANES
                acc = acc + dw_ref[j:j + 1, cols] * gk[s, r0:r0 + rows, :]
            yscr_ref[rc * rows:(rc + 1) * rows, cols] = acc + dwb_ref[:, cols]
        _gate_step(h_ref, (wg0, wg1, wg2, wg3)[k], gate_ref, k)
    for rb in range(tm // tail_rows):
        rs = slice(rb * tail_rows, (rb + 1) * tail_rows)
        y = yscr_ref[rs, :]
        mu = jnp.mean(y, axis=-1, keepdims=True)
        yc = y - mu
        var = jnp.mean(yc * yc, axis=-1, keepdims=True)
        yn = yc * lax.rsqrt(var + EPS) * lg_ref[...] + lb_ref[...]
        out = _dot(_silu(yn).astype(BF16), pw_ref[...])
        y_ref[rs, :] = (out * zs_ref[rs, :]).astype(BF16)


def _branch_specs(m, d, layer, gate_index):
    tm = BRANCH_TM
    once = pl.Buffered(1)

    def w_spec(width, col):
        assert col % width == 0
        return pl.BlockSpec((d, width), lambda i: (0, col // width), pipeline_mode=once)

    def row_spec(width):
        return pl.BlockSpec((tm, width), lambda i: (i, 0))

    gate_col = GATE_COL + gate_index * D_MODEL
    gate_w_specs = [w_spec(GATE_BLOCK, gate_col + k * GATE_BLOCK) for k in range(D_MODEL // GATE_BLOCK)]
    out_shape = (jax.ShapeDtypeStruct((m, POOL_WIDTH), BF16), jax.ShapeDtypeStruct((m, d), BF16))
    params = pltpu.CompilerParams(dimension_semantics=("arbitrary",), vmem_limit_bytes=VMEM_LIMIT)
    return tm, once, w_spec, row_spec, gate_w_specs, out_shape, params


def _pool_branch(h, w_in, pool_w, pool_scale, layer, seq):
    m, d = h.shape
    tm, once, w_spec, row_spec, gate_w_specs, out_shape, params = _branch_specs(m, d, layer, 0)
    return pl.pallas_call(
        functools.partial(_pool_branch_kernel, tiles_per_seq=seq // tm),
        out_shape=out_shape,
        grid=(m // tm,),
        in_specs=[row_spec(d), w_spec(2 * POOL_WIDTH, POOL_COL)] + gate_w_specs + [
            pl.BlockSpec((None, N_POOL_GROUPS, POOL_GROUP, POOL_GROUP), lambda i: (layer, 0, 0, 0),
                         pipeline_mode=once),
            pl.BlockSpec((None, 1, POOL_WIDTH), lambda i: (layer, 0, 0)),
        ],
        out_specs=(row_spec(POOL_WIDTH), row_spec(d)),
        scratch_shapes=[pltpu.VMEM((POOL_HALO, POOL_WIDTH), F32)],
        compiler_params=params,
        name="pool_branch",
    )(h, w_in, w_in, w_in, w_in, w_in, pool_w, pool_scale)


def _attn_branch(h, w_in, attn_sink, layer, seq):
    m, d = h.shape
    tm, once, w_spec, row_spec, gate_w_specs, out_shape, params = _branch_specs(m, d, layer, 1)
    half = ATTN_WIDTH // 2
    return pl.pallas_call(
        functools.partial(_attn_branch_kernel, tiles_per_seq=seq // tm, layer=layer),
        out_shape=out_shape,
        grid=(m // tm,),
        in_specs=[pl.BlockSpec(memory_space=pltpu.SMEM), row_spec(d),
                  w_spec(ATTN_WIDTH, Q_COL), w_spec(2 * KV_WIDTH, KV_COL),
                  w_spec(half, ATTN_Z_COL), w_spec(half, ATTN_Z_COL + half)] + gate_w_specs,
        out_specs=(row_spec(ATTN_WIDTH), row_spec(d)),
        scratch_shapes=[pltpu.VMEM((tm + ATTN_HALO, 2 * KV_WIDTH), BF16)],
        compiler_params=params,
        name="attn_branch",
    )(attn_sink, h, w_in, w_in, w_in, w_in, w_in, w_in, w_in, w_in)


def _conv_branch(h, w_in, conv_dw, conv_dw_b, conv_ln_g, conv_ln_b, conv_pw, layer, seq):
    m, d = h.shape
    tm, once, w_spec, row_spec, gate_w_specs, out_shape, params = _branch_specs(m, d, layer, 2)
    vec = pl.BlockSpec((None, 1, CONV_WIDTH), lambda i: (layer, 0, 0))
    return pl.pallas_call(
        functools.partial(_conv_branch_kernel, tiles_per_seq=seq // tm, rows=64, cb=CONV_CHAIN,
                          tail_rows=256),
        out_shape=out_shape,
        grid=(m // tm,),
        in_specs=[row_spec(d), w_spec(CONV_BLOCK, CONV_COL), w_spec(CONV_BLOCK, CONV_COL + CONV_BLOCK)]
        + gate_w_specs + [
            pl.BlockSpec((None, CONV_KERNEL, CONV_WIDTH), lambda i: (layer, 0, 0)),
            vec, vec, vec,
            pl.BlockSpec((None, CONV_WIDTH, CONV_WIDTH), lambda i: (layer, 0, 0), pipeline_mode=once),
        ],
        out_specs=(row_spec(CONV_WIDTH), row_spec(d)),
        scratch_shapes=[pltpu.VMEM((2, SUBLANES, tm + CONV_HALO, CONV_CHAIN), F32),
                        pltpu.VMEM((CONV_HALO, CONV_WIDTH), F32),
                        pltpu.VMEM((tm, CONV_WIDTH), F32),
                        pltpu.VMEM((tm, CONV_WIDTH), F32)],
        compiler_params=params,
        name="conv_branch",
    )(h, w_in, w_in, w_in, w_in, w_in, w_in, conv_dw, conv_dw_b, conv_ln_g, conv_ln_b, conv_pw)


def _merged_update(x_ref, yp_ref, ya_ref, yc_ref, gp_ref, ga_ref, gc_ref, wp_ref, wa_ref, wc_ref,
                   wo_ref, gate_ref):
    def branch(y_ref, g_ref, w_ref):
        return g_ref[...].astype(F32) * _dot(y_ref[...], w_ref[...])

    merged = branch(yp_ref, gp_ref, wp_ref) + branch(ya_ref, ga_ref, wa_ref) + branch(yc_ref, gc_ref, wc_ref)
    return x_ref[...] + gate_ref[...] * _dot(merged.astype(BF16), wo_ref[...])


def _merge_kernel(x_ref, yp_ref, ya_ref, yc_ref, gp_ref, ga_ref, gc_ref, wp_ref, wa_ref, wc_ref,
                  wo_ref, gate_ref, ng_ref, nsc_ref, nsh_ref, wsrc_ref, o_ref, h_ref, wdst_ref,
                  *, cast_steps):
    xn = _merged_update(x_ref, yp_ref, ya_ref, yc_ref, gp_ref, ga_ref, gc_ref, wp_ref, wa_ref, wc_ref,
                        wo_ref, gate_ref)
    o_ref[...] = xn
    h_ref[...] = _ada_norm(xn, ng_ref[...], nsc_ref[...], nsh_ref[...]).astype(BF16)

    @pl.when(pl.program_id(0) < cast_steps)
    def _():
        wdst_ref[...] = wsrc_ref[...].astype(BF16)


def _merge_last_kernel(x_ref, yp_ref, ya_ref, yc_ref, gp_ref, ga_ref, gc_ref, wp_ref, wa_ref, wc_ref,
                       wo_ref, gate_ref, fg_ref, o_ref):
    xn = _merged_update(x_ref, yp_ref, ya_ref, yc_ref, gp_ref, ga_ref, gc_ref, wp_ref, wa_ref, wc_ref,
                        wo_ref, gate_ref)
    ms = jnp.mean(xn * xn, axis=-1, keepdims=True)
    o_ref[...] = xn * lax.rsqrt(ms + EPS) * fg_ref[...]


def _merge_out(xf, ys, gates, w_bp, w_ba, w_bc, w_out, gate, layer, seq, nxt):
    m, d = xf.shape
    tm = MERGE_TM
    once = pl.Buffered(1)
    row = pl.BlockSpec((tm, d), lambda i: (i, 0))
    ybs = pl.BlockSpec((tm, POOL_WIDTH), lambda i: (i, 0))
    wbs = pl.BlockSpec((None, POOL_WIDTH, d), lambda i: (layer, 0, 0), pipeline_mode=once)
    per_batch = pl.BlockSpec((None, 1, d), lambda i: ((i * tm) // seq, 0, 0))
    in_specs = [row, ybs, ybs, ybs, row, row, row, wbs, wbs, wbs,
                pl.BlockSpec((None, d, d), lambda i: (layer, 0, 0), pipeline_mode=once), per_batch]
    args = [xf, *ys, *gates, w_bp, w_ba, w_bc, w_out, gate]
    params = pltpu.CompilerParams(dimension_semantics=("parallel",), vmem_limit_bytes=VMEM_LIMIT)
    if len(nxt) == 1:
        return pl.pallas_call(
            _merge_last_kernel,
            out_shape=jax.ShapeDtypeStruct((m, d), F32),
            grid=(m // tm,),
            in_specs=in_specs + [pl.BlockSpec((1, d), lambda i: (0, 0))],
            out_specs=row,
            compiler_params=params,
            name="merge_last",
        )(*args, *nxt)
    norm_g, scale, shift, w_in = nxt
    n_in = w_in.shape[-1]
    cast_steps = n_in // CAST_BLOCK
    assert n_in % CAST_BLOCK == 0 and cast_steps <= m // tm
    cast_col = lambda i: jnp.minimum(i, cast_steps - 1)
    return pl.pallas_call(
        functools.partial(_merge_kernel, cast_steps=cast_steps),
        out_shape=(jax.ShapeDtypeStruct((m, d), F32), jax.ShapeDtypeStruct((m, d), BF16),
                   jax.ShapeDtypeStruct((d, n_in), BF16)),
        grid=(m // tm,),
        in_specs=in_specs + [pl.BlockSpec((None, 1, d), lambda i: (layer + 1, 0, 0)), per_batch, per_batch,
                             pl.BlockSpec((None, d, CAST_BLOCK), lambda i: (layer + 1, 0, cast_col(i)))],
        out_specs=(row, row, pl.BlockSpec((d, CAST_BLOCK), lambda i: (0, cast_col(i)))),
        compiler_params=pltpu.CompilerParams(dimension_semantics=("arbitrary",), vmem_limit_bytes=VMEM_LIMIT),
        name="merge_out",
    )(*args, norm_g, scale, shift, w_in)


def kernel(x, c, norm_g, w_ada, b_ada, w_in, pool_w, pool_scale, attn_sink, conv_dw, conv_dw_b,
           conv_ln_g, conv_ln_b, conv_pw, w_branch_pool, w_branch_attn, w_branch_conv, w_out,
           final_g):
    batch, seq, d = x.shape
    depth = w_in.shape[0]
    m = batch * seq
    assert d == D_MODEL and w_in.shape[-1] == IN_WIDTH and batch <= 8
    assert seq % BRANCH_TM == 0 and seq % MERGE_TM == 0

    w_in_b = w_in[0].astype(BF16)
    pool_w_b = pool_w.astype(BF16)
    conv_pw_b = conv_pw.astype(BF16)
    w_bp = w_branch_pool.astype(BF16)
    w_ba = w_branch_attn.astype(BF16)
    w_bc = w_branch_conv.astype(BF16)
    w_o = w_out.astype(BF16)

    c_pad = jnp.zeros((8, d), F32).at[:batch].set(c)
    mod = _ada_mod(c_pad, w_ada, b_ada)
    shifts = [mod[l, :batch, 0:d].reshape(batch, 1, d) for l in range(depth)]
    scales = [mod[l, :batch, d:2 * d].reshape(batch, 1, d) for l in range(depth)]
    gates = [mod[l, :batch, 2 * d:3 * d].reshape(batch, 1, d) for l in range(depth)]

    norm_g3 = norm_g.reshape(depth, 1, d)
    pool_scale3 = pool_scale.reshape(depth, 1, POOL_WIDTH)
    dwb3 = conv_dw_b.reshape(depth, 1, CONV_WIDTH)
    lng3 = conv_ln_g.reshape(depth, 1, CONV_WIDTH)
    lnb3 = conv_ln_b.reshape(depth, 1, CONV_WIDTH)

    xf = x.reshape(m, d)
    h = _first_norm(xf, norm_g3, scales[0], shifts[0], seq)
    for layer in range(depth):
        y_pool, g_pool = _pool_branch(h, w_in_b, pool_w_b, pool_scale3, layer, seq)
        y_attn, g_attn = _attn_branch(h, w_in_b, attn_sink, layer, seq)
        y_conv, g_conv = _conv_branch(h, w_in_b, conv_dw, dwb3, lng3, lnb3, conv_pw_b, layer, seq)
        if layer + 1 < depth:
            nxt = (norm_g3, scales[layer + 1], shifts[layer + 1], w_in)
        else:
            nxt = (final_g.reshape(1, d),)
        res = _merge_out(xf, (y_pool, y_attn, y_conv), (g_pool, g_attn, g_conv), w_bp, w_ba, w_bc, w_o,
                         gates[layer], layer, seq, nxt)
        if layer + 1 < depth:
            xf, h, w_in_b = res
        else:
            xf = res
    return xf.reshape(batch, seq, d)
```

```python
import functools

import jax
import jax.numpy as jnp
from jax import lax
from jax.experimental import pallas as pl
from jax.experimental.pallas import tpu as pltpu

D_MODEL = 2048
CHUNK = 64
EPS = 1e-6
POOL_WIDTH = 1024
POOL_WINDOWS = (2, 4, 8, 16)
N_POOL_GROUPS = 4
POOL_GROUP = POOL_WIDTH // N_POOL_GROUPS
N_Q_HEADS = 16
N_KV_HEADS = 4
HEAD_DIM = 64
GROUP = N_Q_HEADS // N_KV_HEADS
ATTN_WIDTH = N_Q_HEADS * HEAD_DIM
KV_WIDTH = N_KV_HEADS * HEAD_DIM
WINDOW_CHUNKS = 2
N_KEYS = (WINDOW_CHUNKS + 1) * CHUNK
CONV_WIDTH = 1024
CONV_KERNEL = 31
IN_WIDTH = 2 * POOL_WIDTH + 2 * ATTN_WIDTH + 2 * KV_WIDTH + 3 * CONV_WIDTH + 3 * D_MODEL

POOL_COL = 0
Q_COL = 2 * POOL_WIDTH
KV_COL = Q_COL + ATTN_WIDTH
ATTN_Z_COL = KV_COL + 2 * KV_WIDTH
CONV_COL = ATTN_Z_COL + ATTN_WIDTH
GATE_COL = CONV_COL + 3 * CONV_WIDTH
GATE_BLOCK = 512
CONV_BLOCK = 1536
CONV_CHAIN = 256
CAST_BLOCK = 512

POOL_HALO = 16
CONV_HALO = 32
ATTN_HALO = WINDOW_CHUNKS * CHUNK
SUBLANES = 8
BF16_ROWS = 16

BRANCH_TM = 512
POOL_TM = 1024
MERGE_TM = 256
VMEM_LIMIT = 56 * 1024 * 1024

F32 = jnp.float32
BF16 = jnp.bfloat16


def _sigmoid(v):
    return jax.nn.sigmoid(v)


def _silu(v):
    return v * jax.nn.sigmoid(v)


def _dot(a, b):
    return jnp.dot(a, b, preferred_element_type=F32)


def _ada_norm(x, g, scale, shift):
    ms = jnp.mean(x * x, axis=-1, keepdims=True)
    return x * lax.rsqrt(ms + EPS) * g * (1.0 + scale) + shift


def _ada_kernel(c_ref, w_ref, b_ref, o_ref):
    ca = _silu(c_ref[...])
    o_ref[...] = jnp.dot(ca, w_ref[...], preferred_element_type=F32,
                         precision=lax.Precision.HIGHEST) + b_ref[...]


def _ada_mod(c_pad, w_ada, b_ada):
    depth, d, n = w_ada.shape
    tn = 2048
    return pl.pallas_call(
        _ada_kernel,
        out_shape=jax.ShapeDtypeStruct((depth, c_pad.shape[0], n), F32),
        grid=(depth, n // tn),
        in_specs=[
            pl.BlockSpec(c_pad.shape, lambda l, j: (0, 0)),
            pl.BlockSpec((None, d, tn), lambda l, j: (l, 0, j)),
            pl.BlockSpec((None, 1, tn), lambda l, j: (l, 0, j)),
        ],
        out_specs=pl.BlockSpec((None, c_pad.shape[0], tn), lambda l, j: (l, 0, j)),
        compiler_params=pltpu.CompilerParams(dimension_semantics=("parallel", "parallel")),
        name="ada_mod",
    )(c_pad, w_ada, b_ada.reshape(depth, 1, n))


def _norm_kernel(x_ref, g_ref, sc_ref, sh_ref, h_ref):
    h_ref[...] = _ada_norm(x_ref[...], g_ref[...], sc_ref[...], sh_ref[...]).astype(BF16)


def _first_norm(xf, norm_g, scale, shift, seq):
    m, d = xf.shape
    tm = 512
    vec = lambda i: ((i * tm) // seq, 0, 0)
    return pl.pallas_call(
        _norm_kernel,
        out_shape=jax.ShapeDtypeStruct((m, d), BF16),
        grid=(m // tm,),
        in_specs=[pl.BlockSpec((tm, d), lambda i: (i, 0)),
                  pl.BlockSpec((None, 1, d), lambda i: (0, 0, 0)),
                  pl.BlockSpec((None, 1, d), vec),
                  pl.BlockSpec((None, 1, d), vec)],
        out_specs=pl.BlockSpec((tm, d), lambda i: (i, 0)),
        compiler_params=pltpu.CompilerParams(dimension_semantics=("parallel",)),
        name="first_norm",
    )(xf, norm_g, scale, shift)


def _gate_step(h_ref, wg_ref, gate_ref, k):
    gate_ref[:, k * GATE_BLOCK:(k + 1) * GATE_BLOCK] = _sigmoid(_dot(h_ref[...], wg_ref[...])).astype(BF16)


def _pool_branch_kernel(h_ref, wp_ref, wg0, wg1, wg2, wg3, pw_ref, ps_ref, *rest, tiles_per_seq, n_cast):
    src_refs = rest[:n_cast]
    y_ref, gate_ref = rest[n_cast:n_cast + 2]
    dst_refs = rest[n_cast + 2:2 * n_cast + 2]
    carry_ref = rest[2 * n_cast + 2]
    tm = h_ref.shape[0]
    ti = pl.program_id(0) % tiles_per_seq
    for src_ref, dst_ref in zip(src_refs, dst_refs):
        dst_ref[...] = src_ref[...].astype(BF16)

    @pl.when(pl.program_id(0) == 0)
    def _():
        carry_ref[...] = jnp.zeros(carry_ref.shape, F32)

    t = ti * tm + lax.broadcasted_iota(jnp.int32, (tm, POOL_GROUP), 0)
    for gi, w in enumerate(POOL_WINDOWS):
        cols = slice(gi * POOL_GROUP, (gi + 1) * POOL_GROUP)
        zcols = slice(POOL_WIDTH + gi * POOL_GROUP, POOL_WIDTH + (gi + 1) * POOL_GROUP)
        ug = _dot(h_ref[...], wp_ref[:, cols])
        zg = _dot(h_ref[...], wp_ref[:, zcols])
        halo = jnp.where(ti == 0, 0.0, carry_ref[:, cols])
        carry_ref[:, cols] = ug[tm - POOL_HALO:, :]
        s = jnp.concatenate([halo, ug], axis=0)
        k = 1
        while k < w:
            s = s[k:] + s[:-k]
            k *= 2
        off = POOL_HALO - (w - 1)
        s = s[off:off + tm]
        cnt = jnp.minimum(t + 1, w).astype(F32)
        mixed = (s / cnt - ug).astype(BF16)
        y = _dot(mixed, pw_ref[gi]) * ps_ref[:, cols] * _silu(zg)
        y_ref[:, cols] = y.astype(BF16)
        _gate_step(h_ref, (wg0, wg1, wg2, wg3)[gi], gate_ref, gi)


def _attn_branch_kernel(sink_ref, h_ref, wq_ref, wkv_ref, wz0_ref, wz1_ref, wg0, wg1, wg2, wg3,
                        y_ref, gate_ref, kvx_ref, *, tiles_per_seq, layer):
    tm = h_ref.shape[0]
    ti = pl.program_id(0) % tiles_per_seq
    head_cols = GROUP * HEAD_DIM

    @pl.when(pl.program_id(0) == 0)
    def _():
        kvx_ref[0:ATTN_HALO, :] = jnp.zeros((ATTN_HALO, 2 * KV_WIDTH), BF16)

    kvx_ref[ATTN_HALO:, :] = _dot(h_ref[...], wkv_ref[...]).astype(BF16)
    row_head = lax.broadcasted_iota(jnp.int32, (GROUP * CHUNK, 1), 0) // CHUNK
    key_chunk = lax.broadcasted_iota(jnp.int32, (1, N_KEYS), 1) // CHUNK
    for kvh in range(N_KV_HEADS):
        cols = slice(kvh * head_cols, (kvh + 1) * head_cols)
        q = _dot(h_ref[...], wq_ref[:, cols]).astype(BF16)
        wz_ref = wz0_ref if kvh < N_KV_HEADS // 2 else wz1_ref
        zoff = (kvh % (N_KV_HEADS // 2)) * head_cols
        sz = _silu(_dot(h_ref[...], wz_ref[:, zoff:zoff + head_cols]))
        heads = [kvh * GROUP + g for g in range(GROUP)]
        sk = jnp.full((GROUP * CHUNK, 1), sink_ref[layer, heads[GROUP - 1]], F32)
        for g in range(GROUP - 2, -1, -1):
            sk = jnp.where(row_head == g, sink_ref[layer, heads[g]], sk)
        for c in range(tm // CHUNK):
            r0 = c * CHUNK
            qc = q[r0:r0 + CHUNK, :]
            qs = jnp.concatenate([qc[:, g * HEAD_DIM:(g + 1) * HEAD_DIM] for g in range(GROUP)], axis=0)
            kk = kvx_ref[r0:r0 + N_KEYS, kvh * HEAD_DIM:(kvh + 1) * HEAD_DIM]
            vv = kvx_ref[r0:r0 + N_KEYS, KV_WIDTH + kvh * HEAD_DIM:KV_WIDTH + (kvh + 1) * HEAD_DIM]
            n = ti * (tm // CHUNK) + c
            bias = jnp.where(n - WINDOW_CHUNKS + key_chunk >= 0, 0.0, -jnp.inf).astype(F32)
            s = lax.dot_general(qs, kk, (((1,), (1,)), ((), ())), preferred_element_type=F32)
            s = s * (HEAD_DIM ** -0.5) + bias
            mx = jnp.maximum(jnp.max(s, axis=-1, keepdims=True), sk)
            p = jnp.exp(s - mx)
            denom = jnp.sum(p, axis=-1, keepdims=True) + jnp.exp(sk - mx)
            o = _dot(p.astype(BF16), vv) / denom
            oc = jnp.concatenate([o[g * CHUNK:(g + 1) * CHUNK] for g in range(GROUP)], axis=1)
            y_ref[r0:r0 + CHUNK, cols] = (oc * sz[r0:r0 + CHUNK, :]).astype(BF16)
        _gate_step(h_ref, (wg0, wg1, wg2, wg3)[kvh], gate_ref, kvh)
    kvx_ref[0:ATTN_HALO, :] = kvx_ref[tm:tm + ATTN_HALO, :]


def _conv_branch_kernel(h_ref, w0_ref, w1_ref, wg0, wg1, wg2, wg3, dw_ref, dwb_ref, lg_ref, lb_ref,
                        pw_ref, y_ref, gate_ref, gs_ref, carry_ref, yscr_ref, zs_ref,
                        *, tiles_per_seq, rows, cb, tail_rows):
    tm = h_ref.shape[0]
    ext = tm + CONV_HALO
    base = CONV_HALO - (CONV_KERNEL - 1)
    ti = pl.program_id(0) % tiles_per_seq

    @pl.when(pl.program_id(0) == 0)
    def _():
        carry_ref[...] = jnp.zeros(carry_ref.shape, F32)

    def wcols(c0):
        if c0 < CONV_BLOCK:
            return w0_ref[:, c0:c0 + cb]
        return w1_ref[:, c0 - CONV_BLOCK:c0 - CONV_BLOCK + cb]

    for k in range(CONV_WIDTH // cb):
        cols = slice(k * cb, (k + 1) * cb)
        gk = gs_ref.at[k % 2]
        a = _dot(h_ref[...], wcols(k * cb))
        b = _dot(h_ref[...], wcols(CONV_WIDTH + k * cb))
        z = _dot(h_ref[...], wcols(2 * CONV_WIDTH + k * cb))
        g = a * _sigmoid(b)
        gk[0, 0:CONV_HALO, :] = jnp.where(ti == 0, 0.0, carry_ref[:, cols])
        gk[0, CONV_HALO:, :] = g
        carry_ref[:, cols] = g[tm - CONV_HALO:, :]
        zs_ref[:, cols] = _silu(z)
        for s in range(1, SUBLANES):
            gk[s, 0:ext - SUBLANES, :] = gk[0, s:s + ext - SUBLANES, :]
        for rc in range(tm // rows):
            acc = jnp.zeros((rows, cb), F32)
            for j in range(CONV_KERNEL):
                q, s = divmod(base + j, SUBLANES)
                r0 = rc * rows + q * SUBLANES
                acc = acc + dw_ref[j:j + 1, cols] * gk[s, r0:r0 + rows, :]
            yscr_ref[rc * rows:(rc + 1) * rows, cols] = acc + dwb_ref[:, cols]
        _gate_step(h_ref, (wg0, wg1, wg2, wg3)[k], gate_ref, k)
    for rb in range(tm // tail_rows):
        rs = slice(rb * tail_rows, (rb + 1) * tail_rows)
        y = yscr_ref[rs, :]
        mu = jnp.mean(y, axis=-1, keepdims=True)
        yc = y - mu
        var = jnp.mean(yc * yc, axis=-1, keepdims=True)
        yn = yc * lax.rsqrt(var + EPS) * lg_ref[...] + lb_ref[...]
        out = _dot(_silu(yn).astype(BF16), pw_ref[...])
        y_ref[rs, :] = (out * zs_ref[rs, :]).astype(BF16)


def _branch_specs(m, d, layer, gate_index, tm=BRANCH_TM):
    once = pl.Buffered(1)

    def w_spec(width, col):
        assert col % width == 0
        return pl.BlockSpec((d, width), lambda i: (0, col // width), pipeline_mode=once)

    def row_spec(width):
        return pl.BlockSpec((tm, width), lambda i: (i, 0))

    gate_col = GATE_COL + gate_index * D_MODEL
    gate_w_specs = [w_spec(GATE_BLOCK, gate_col + k * GATE_BLOCK) for k in range(D_MODEL // GATE_BLOCK)]
    out_shape = (jax.ShapeDtypeStruct((m, POOL_WIDTH), BF16), jax.ShapeDtypeStruct((m, d), BF16))
    params = pltpu.CompilerParams(dimension_semantics=("arbitrary",), vmem_limit_bytes=VMEM_LIMIT)
    return tm, once, w_spec, row_spec, gate_w_specs, out_shape, params


def _pool_branch(h, w_in, pool_w, pool_scale, layer, seq, next_weights):
    m, d = h.shape
    tm, once, w_spec, row_spec, gate_w_specs, out_shape, params = _branch_specs(m, d, layer, 0, POOL_TM)
    steps = m // tm
    src_specs, dst_specs, dst_shapes = [], [], []
    for w in next_weights:
        rows, cols = w.shape[1:]
        rb = rows // steps
        assert rows % steps == 0 and rb % BF16_ROWS == 0
        src_specs.append(pl.BlockSpec((None, rb, cols), lambda i: (layer + 1, i, 0)))
        dst_specs.append(pl.BlockSpec((rb, cols), lambda i: (i, 0)))
        dst_shapes.append(jax.ShapeDtypeStruct((rows, cols), BF16))
    return pl.pallas_call(
        functools.partial(_pool_branch_kernel, tiles_per_seq=seq // tm, n_cast=len(next_weights)),
        out_shape=(*out_shape, *dst_shapes),
        grid=(steps,),
        in_specs=[row_spec(d), w_spec(2 * POOL_WIDTH, POOL_COL)] + gate_w_specs + [
            pl.BlockSpec((None, N_POOL_GROUPS, POOL_GROUP, POOL_GROUP), lambda i: (layer, 0, 0, 0),
                         pipeline_mode=once),
            pl.BlockSpec((None, 1, POOL_WIDTH), lambda i: (layer, 0, 0)),
        ] + src_specs,
        out_specs=(row_spec(POOL_WIDTH), row_spec(d), *dst_specs),
        scratch_shapes=[pltpu.VMEM((POOL_HALO, POOL_WIDTH), F32)],
        compiler_params=params,
        name="pool_branch",
    )(h, w_in, w_in, w_in, w_in, w_in, pool_w, pool_scale, *next_weights)


def _attn_branch(h, w_in, attn_sink, layer, seq):
    m, d = h.shape
    tm, once, w_spec, row_spec, gate_w_specs, out_shape, params = _branch_specs(m, d, layer, 1)
    half = ATTN_WIDTH // 2
    return pl.pallas_call(
        functools.partial(_attn_branch_kernel, tiles_per_seq=seq // tm, layer=layer),
        out_shape=out_shape,
        grid=(m // tm,),
        in_specs=[pl.BlockSpec(memory_space=pltpu.SMEM), row_spec(d),
                  w_spec(ATTN_WIDTH, Q_COL), w_spec(2 * KV_WIDTH, KV_COL),
                  w_spec(half, ATTN_Z_COL), w_spec(half, ATTN_Z_COL + half)] + gate_w_specs,
        out_specs=(row_spec(ATTN_WIDTH), row_spec(d)),
        scratch_shapes=[pltpu.VMEM((tm + ATTN_HALO, 2 * KV_WIDTH), BF16)],
        compiler_params=params,
        name="attn_branch",
    )(attn_sink, h, w_in, w_in, w_in, w_in, w_in, w_in, w_in, w_in)


def _conv_branch(h, w_in, conv_dw, conv_dw_b, conv_ln_g, conv_ln_b, conv_pw, layer, seq):
    m, d = h.shape
    tm, once, w_spec, row_spec, gate_w_specs, out_shape, params = _branch_specs(m, d, layer, 2)
    vec = pl.BlockSpec((None, 1, CONV_WIDTH), lambda i: (layer, 0, 0))
    return pl.pallas_call(
        functools.partial(_conv_branch_kernel, tiles_per_seq=seq // tm, rows=64, cb=CONV_CHAIN,
                          tail_rows=256),
        out_shape=out_shape,
        grid=(m // tm,),
        in_specs=[row_spec(d), w_spec(CONV_BLOCK, CONV_COL), w_spec(CONV_BLOCK, CONV_COL + CONV_BLOCK)]
        + gate_w_specs + [
            pl.BlockSpec((None, CONV_KERNEL, CONV_WIDTH), lambda i: (layer, 0, 0)),
            vec, vec, vec,
            pl.BlockSpec((None, CONV_WIDTH, CONV_WIDTH), lambda i: (layer, 0, 0), pipeline_mode=once),
        ],
        out_specs=(row_spec(CONV_WIDTH), row_spec(d)),
        scratch_shapes=[pltpu.VMEM((2, SUBLANES, tm + CONV_HALO, CONV_CHAIN), F32),
                        pltpu.VMEM((CONV_HALO, CONV_WIDTH), F32),
                        pltpu.VMEM((tm, CONV_WIDTH), F32),
                        pltpu.VMEM((tm, CONV_WIDTH), F32)],
        compiler_params=params,
        name="conv_branch",
    )(h, w_in, w_in, w_in, w_in, w_in, w_in, conv_dw, conv_dw_b, conv_ln_g, conv_ln_b, conv_pw)


def _merged_update(x_ref, yp_ref, ya_ref, yc_ref, gp_ref, ga_ref, gc_ref, wp_ref, wa_ref, wc_ref,
                   wo_ref, gate_ref):
    def branch(y_ref, g_ref, w_ref):
        return g_ref[...].astype(F32) * _dot(y_ref[...], w_ref[...])

    merged = branch(yp_ref, gp_ref, wp_ref) + branch(ya_ref, ga_ref, wa_ref) + branch(yc_ref, gc_ref, wc_ref)
    return x_ref[...] + gate_ref[...] * _dot(merged.astype(BF16), wo_ref[...])


def _merge_kernel(x_ref, yp_ref, ya_ref, yc_ref, gp_ref, ga_ref, gc_ref, wp_ref, wa_ref, wc_ref,
                  wo_ref, gate_ref, ng_ref, nsc_ref, nsh_ref, wsrc_ref, o_ref, h_ref, wdst_ref,
                  *, cast_steps):
    xn = _merged_update(x_ref, yp_ref, ya_ref, yc_ref, gp_ref, ga_ref, gc_ref, wp_ref, wa_ref, wc_ref,
                        wo_ref, gate_ref)
    o_ref[...] = xn
    h_ref[...] = _ada_norm(xn, ng_ref[...], nsc_ref[...], nsh_ref[...]).astype(BF16)

    @pl.when(pl.program_id(0) < cast_steps)
    def _():
        wdst_ref[...] = wsrc_ref[...].astype(BF16)


def _merge_last_kernel(x_ref, yp_ref, ya_ref, yc_ref, gp_ref, ga_ref, gc_ref, wp_ref, wa_ref, wc_ref,
                       wo_ref, gate_ref, fg_ref, o_ref):
    xn = _merged_update(x_ref, yp_ref, ya_ref, yc_ref, gp_ref, ga_ref, gc_ref, wp_ref, wa_ref, wc_ref,
                        wo_ref, gate_ref)
    ms = jnp.mean(xn * xn, axis=-1, keepdims=True)
    o_ref[...] = xn * lax.rsqrt(ms + EPS) * fg_ref[...]


def _merge_out(xf, ys, gates, w_bp, w_ba, w_bc, w_out, gate, layer, seq, nxt):
    m, d = xf.shape
    tm = MERGE_TM
    once = pl.Buffered(1)
    row = pl.BlockSpec((tm, d), lambda i: (i, 0))
    ybs = pl.BlockSpec((tm, POOL_WIDTH), lambda i: (i, 0))
    wbs = pl.BlockSpec((POOL_WIDTH, d), lambda i: (0, 0), pipeline_mode=once)
    per_batch = pl.BlockSpec((None, 1, d), lambda i: ((i * tm) // seq, 0, 0))
    in_specs = [row, ybs, ybs, ybs, row, row, row, wbs, wbs, wbs,
                pl.BlockSpec((d, d), lambda i: (0, 0), pipeline_mode=once), per_batch]
    args = [xf, *ys, *gates, w_bp, w_ba, w_bc, w_out, gate]
    params = pltpu.CompilerParams(dimension_semantics=("parallel",), vmem_limit_bytes=VMEM_LIMIT)
    if len(nxt) == 1:
        return pl.pallas_call(
            _merge_last_kernel,
            out_shape=jax.ShapeDtypeStruct((m, d), F32),
            grid=(m // tm,),
            in_specs=in_specs + [pl.BlockSpec((1, d), lambda i: (0, 0))],
            out_specs=row,
            compiler_params=params,
            name="merge_last",
        )(*args, *nxt)
    norm_g, scale, shift, w_in = nxt
    n_in = w_in.shape[-1]
    cast_steps = n_in // CAST_BLOCK
    assert n_in % CAST_BLOCK == 0 and cast_steps <= m // tm
    cast_col = lambda i: jnp.minimum(i, cast_steps - 1)
    return pl.pallas_call(
        functools.partial(_merge_kernel, cast_steps=cast_steps),
        out_shape=(jax.ShapeDtypeStruct((m, d), F32), jax.ShapeDtypeStruct((m, d), BF16),
                   jax.ShapeDtypeStruct((d, n_in), BF16)),
        grid=(m // tm,),
        in_specs=in_specs + [pl.BlockSpec((None, 1, d), lambda i: (layer + 1, 0, 0)), per_batch, per_batch,
                             pl.BlockSpec((None, d, CAST_BLOCK), lambda i: (layer + 1, 0, cast_col(i)))],
        out_specs=(row, row, pl.BlockSpec((d, CAST_BLOCK), lambda i: (0, cast_col(i)))),
        compiler_params=pltpu.CompilerParams(dimension_semantics=("arbitrary",), vmem_limit_bytes=VMEM_LIMIT),
        name="merge_out",
    )(*args, norm_g, scale, shift, w_in)


def kernel(x, c, norm_g, w_ada, b_ada, w_in, pool_w, pool_scale, attn_sink, conv_dw, conv_dw_b,
           conv_ln_g, conv_ln_b, conv_pw, w_branch_pool, w_branch_attn, w_branch_conv, w_out,
           final_g):
    batch, seq, d = x.shape
    depth = w_in.shape[0]
    m = batch * seq
    assert d == D_MODEL and w_in.shape[-1] == IN_WIDTH and batch <= 8
    assert seq % BRANCH_TM == 0 and seq % MERGE_TM == 0 and seq % POOL_TM == 0

    w_in_b = w_in[0].astype(BF16)
    pool_w_b = pool_w.astype(BF16)
    conv_pw_b = conv_pw.astype(BF16)
    merge_w = (w_branch_pool, w_branch_attn, w_branch_conv, w_out)
    merge_w_b = tuple(w[0].astype(BF16) for w in merge_w)

    c_pad = jnp.zeros((8, d), F32).at[:batch].set(c)
    mod = _ada_mod(c_pad, w_ada, b_ada)
    shifts = [mod[l, :batch, 0:d].reshape(batch, 1, d) for l in range(depth)]
    scales = [mod[l, :batch, d:2 * d].reshape(batch, 1, d) for l in range(depth)]
    gates = [mod[l, :batch, 2 * d:3 * d].reshape(batch, 1, d) for l in range(depth)]

    norm_g3 = norm_g.reshape(depth, 1, d)
    pool_scale3 = pool_scale.reshape(depth, 1, POOL_WIDTH)
    dwb3 = conv_dw_b.reshape(depth, 1, CONV_WIDTH)
    lng3 = conv_ln_g.reshape(depth, 1, CONV_WIDTH)
    lnb3 = conv_ln_b.reshape(depth, 1, CONV_WIDTH)

    xf = x.reshape(m, d)
    h = _first_norm(xf, norm_g3, scales[0], shifts[0], seq)
    for layer in range(depth):
        last = layer + 1 == depth
        y_pool, g_pool, *next_merge_w = _pool_branch(h, w_in_b, pool_w_b, pool_scale3, layer, seq,
                                                     () if last else merge_w)
        y_attn, g_attn = _attn_branch(h, w_in_b, attn_sink, layer, seq)
        y_conv, g_conv = _conv_branch(h, w_in_b, conv_dw, dwb3, lng3, lnb3, conv_pw_b, layer, seq)
        if last:
            nxt = (final_g.reshape(1, d),)
        else:
            nxt = (norm_g3, scales[layer + 1], shifts[layer + 1], w_in)
        res = _merge_out(xf, (y_pool, y_attn, y_conv), (g_pool, g_attn, g_conv), *merge_w_b,
                         gates[layer], layer, seq, nxt)
        if last:
            xf = res
        else:
            xf, h, w_in_b = res
            merge_w_b = tuple(next_merge_w)
    return xf.reshape(batch, seq, d)
```

```python
import functools

import jax
import jax.numpy as jnp
from jax import lax
from jax.experimental import pallas as pl
from jax.experimental.pallas import tpu as pltpu

D_MODEL = 2048
CHUNK = 64
EPS = 1e-6
POOL_WIDTH = 1024
POOL_WINDOWS = (2, 4, 8, 16)
N_POOL_GROUPS = 4
POOL_GROUP = POOL_WIDTH // N_POOL_GROUPS
N_Q_HEADS = 16
N_KV_HEADS = 4
HEAD_DIM = 64
GROUP = N_Q_HEADS // N_KV_HEADS
ATTN_WIDTH = N_Q_HEADS * HEAD_DIM
KV_WIDTH = N_KV_HEADS * HEAD_DIM
WINDOW_CHUNKS = 2
N_KEYS = (WINDOW_CHUNKS + 1) * CHUNK
CONV_WIDTH = 1024
CONV_KERNEL = 31
IN_WIDTH = 2 * POOL_WIDTH + 2 * ATTN_WIDTH + 2 * KV_WIDTH + 3 * CONV_WIDTH + 3 * D_MODEL

POOL_COL = 0
Q_COL = 2 * POOL_WIDTH
KV_COL = Q_COL + ATTN_WIDTH
ATTN_Z_COL = KV_COL + 2 * KV_WIDTH
CONV_COL = ATTN_Z_COL + ATTN_WIDTH
GATE_COL = CONV_COL + 3 * CONV_WIDTH
GATE_BLOCK = 512
CONV_BLOCK = 1536
CONV_CHAIN = 256
CAST_BLOCK = 512

POOL_HALO = 16
CONV_HALO = 32
ATTN_HALO = WINDOW_CHUNKS * CHUNK
SUBLANES = 8
BF16_ROWS = 16

BRANCH_TM = 512
POOL_TM = 1024
MERGE_TM = 256
NORM_TM = 512
ADA_TN = 2048
VMEM_LIMIT = 56 * 1024 * 1024

F32 = jnp.float32
BF16 = jnp.bfloat16


def _sigmoid(v):
    return jax.nn.sigmoid(v)


def _silu(v):
    return v * jax.nn.sigmoid(v)


def _dot(a, b):
    return jnp.dot(a, b, preferred_element_type=F32)


def _ada_norm(x, g, scale, shift):
    ms = jnp.mean(x * x, axis=-1, keepdims=True)
    return x * lax.rsqrt(ms + EPS) * g * (1.0 + scale) + shift


def _ada_kernel(c_ref, w_ref, b_ref, o_ref):
    ca = _silu(c_ref[...])
    o_ref[...] = jnp.dot(ca, w_ref[...], preferred_element_type=F32,
                         precision=lax.Precision.HIGHEST) + b_ref[...]


def _ada_mod(c_pad, w_ada, b_ada):
    depth, d, n = w_ada.shape
    tn = ADA_TN
    return pl.pallas_call(
        _ada_kernel,
        out_shape=jax.ShapeDtypeStruct((depth, c_pad.shape[0], n), F32),
        grid=(depth, n // tn),
        in_specs=[
            pl.BlockSpec(c_pad.shape, lambda l, j: (0, 0)),
            pl.BlockSpec((None, d, tn), lambda l, j: (l, 0, j)),
            pl.BlockSpec((None, 1, tn), lambda l, j: (l, 0, j)),
        ],
        out_specs=pl.BlockSpec((None, c_pad.shape[0], tn), lambda l, j: (l, 0, j)),
        compiler_params=pltpu.CompilerParams(dimension_semantics=("parallel", "parallel")),
        name="ada_mod",
    )(c_pad, w_ada, b_ada.reshape(depth, 1, n))


def _norm_kernel(x_ref, g_ref, sc_ref, sh_ref, h_ref):
    h_ref[...] = _ada_norm(x_ref[...], g_ref[...], sc_ref[...], sh_ref[...]).astype(BF16)


def _first_norm(xf, norm_g, scale, shift, seq):
    m, d = xf.shape
    tm = NORM_TM
    vec = lambda i: ((i * tm) // seq, 0, 0)
    return pl.pallas_call(
        _norm_kernel,
        out_shape=jax.ShapeDtypeStruct((m, d), BF16),
        grid=(m // tm,),
        in_specs=[pl.BlockSpec((tm, d), lambda i: (i, 0)),
                  pl.BlockSpec((None, 1, d), lambda i: (0, 0, 0)),
                  pl.BlockSpec((None, 1, d), vec),
                  pl.BlockSpec((None, 1, d), vec)],
        out_specs=pl.BlockSpec((tm, d), lambda i: (i, 0)),
        compiler_params=pltpu.CompilerParams(dimension_semantics=("parallel",)),
        name="first_norm",
    )(xf, norm_g, scale, shift)


def _gate_step(h_ref, wg_ref, gate_ref, k):
    gate_ref[:, k * GATE_BLOCK:(k + 1) * GATE_BLOCK] = _sigmoid(_dot(h_ref[...], wg_ref[...])).astype(BF16)


def _pool_branch_kernel(h_ref, wp_ref, wg0, wg1, wg2, wg3, pw_ref, ps_ref, *rest, tiles_per_seq, n_cast):
    src_refs = rest[:n_cast]
    y_ref, gate_ref = rest[n_cast:n_cast + 2]
    dst_refs = rest[n_cast + 2:2 * n_cast + 2]
    carry_ref = rest[2 * n_cast + 2]
    tm = h_ref.shape[0]
    ti = pl.program_id(0) % tiles_per_seq
    for src_ref, dst_ref in zip(src_refs, dst_refs):
        dst_ref[...] = src_ref[...].astype(BF16)

    @pl.when(pl.program_id(0) == 0)
    def _():
        carry_ref[...] = jnp.zeros(carry_ref.shape, F32)

    t = ti * tm + lax.broadcasted_iota(jnp.int32, (tm, POOL_GROUP), 0)
    for gi, w in enumerate(POOL_WINDOWS):
        cols = slice(gi * POOL_GROUP, (gi + 1) * POOL_GROUP)
        zcols = slice(POOL_WIDTH + gi * POOL_GROUP, POOL_WIDTH + (gi + 1) * POOL_GROUP)
        ug = _dot(h_ref[...], wp_ref[:, cols])
        zg = _dot(h_ref[...], wp_ref[:, zcols])
        halo = jnp.where(ti == 0, 0.0, carry_ref[:, cols])
        carry_ref[:, cols] = ug[tm - POOL_HALO:, :]
        s = jnp.concatenate([halo, ug], axis=0)
        k = 1
        while k < w:
            s = s[k:] + s[:-k]
            k *= 2
        off = POOL_HALO - (w - 1)
        s = s[off:off + tm]
        cnt = jnp.minimum(t + 1, w).astype(F32)
        mixed = (s / cnt - ug).astype(BF16)
        y = _dot(mixed, pw_ref[gi]) * ps_ref[:, cols] * _silu(zg)
        y_ref[:, cols] = y.astype(BF16)
        _gate_step(h_ref, (wg0, wg1, wg2, wg3)[gi], gate_ref, gi)


def _attn_branch_kernel(sink_ref, h_ref, wq_ref, wkv_ref, wz0_ref, wz1_ref, wg0, wg1, wg2, wg3,
                        y_ref, gate_ref, kvx_ref, *, tiles_per_seq, layer):
    tm = h_ref.shape[0]
    ti = pl.program_id(0) % tiles_per_seq
    head_cols = GROUP * HEAD_DIM

    @pl.when(pl.program_id(0) == 0)
    def _():
        kvx_ref[0:ATTN_HALO, :] = jnp.zeros((ATTN_HALO, 2 * KV_WIDTH), BF16)

    kvx_ref[ATTN_HALO:, :] = _dot(h_ref[...], wkv_ref[...]).astype(BF16)
    row_head = lax.broadcasted_iota(jnp.int32, (GROUP * CHUNK, 1), 0) // CHUNK
    key_chunk = lax.broadcasted_iota(jnp.int32, (1, N_KEYS), 1) // CHUNK
    for kvh in range(N_KV_HEADS):
        cols = slice(kvh * head_cols, (kvh + 1) * head_cols)
        q = _dot(h_ref[...], wq_ref[:, cols]).astype(BF16)
        wz_ref = wz0_ref if kvh < N_KV_HEADS // 2 else wz1_ref
        zoff = (kvh % (N_KV_HEADS // 2)) * head_cols
        sz = _silu(_dot(h_ref[...], wz_ref[:, zoff:zoff + head_cols]))
        heads = [kvh * GROUP + g for g in range(GROUP)]
        sk = jnp.full((GROUP * CHUNK, 1), sink_ref[layer, heads[GROUP - 1]], F32)
        for g in range(GROUP - 2, -1, -1):
            sk = jnp.where(row_head == g, sink_ref[layer, heads[g]], sk)
        for c in range(tm // CHUNK):
            r0 = c * CHUNK
            qc = q[r0:r0 + CHUNK, :]
            qs = jnp.concatenate([qc[:, g * HEAD_DIM:(g + 1) * HEAD_DIM] for g in range(GROUP)], axis=0)
            kk = kvx_ref[r0:r0 + N_KEYS, kvh * HEAD_DIM:(kvh + 1) * HEAD_DIM]
            vv = kvx_ref[r0:r0 + N_KEYS, KV_WIDTH + kvh * HEAD_DIM:KV_WIDTH + (kvh + 1) * HEAD_DIM]
            n = ti * (tm // CHUNK) + c
            bias = jnp.where(n - WINDOW_CHUNKS + key_chunk >= 0, 0.0, -jnp.inf).astype(F32)
            s = lax.dot_general(qs, kk, (((1,), (1,)), ((), ())), preferred_element_type=F32)
            s = s * (HEAD_DIM ** -0.5) + bias
            mx = jnp.maximum(jnp.max(s, axis=-1, keepdims=True), sk)
            p = jnp.exp(s - mx)
            denom = jnp.sum(p, axis=-1, keepdims=True) + jnp.exp(sk - mx)
            o = _dot(p.astype(BF16), vv) / denom
            oc = jnp.concatenate([o[g * CHUNK:(g + 1) * CHUNK] for g in range(GROUP)], axis=1)
            y_ref[r0:r0 + CHUNK, cols] = (oc * sz[r0:r0 + CHUNK, :]).astype(BF16)
        _gate_step(h_ref, (wg0, wg1, wg2, wg3)[kvh], gate_ref, kvh)
    kvx_ref[0:ATTN_HALO, :] = kvx_ref[tm:tm + ATTN_HALO, :]


def _conv_branch_kernel(h_ref, w0_ref, w1_ref, wg0, wg1, wg2, wg3, dw_ref, dwb_ref, lg_ref, lb_ref,
                        pw_ref, y_ref, gate_ref, gs_ref, carry_ref, yscr_ref, zs_ref,
                        *, tiles_per_seq, rows, cb, tail_rows):
    tm = h_ref.shape[0]
    ext = tm + CONV_HALO
    base = CONV_HALO - (CONV_KERNEL - 1)
    ti = pl.program_id(0) % tiles_per_seq

    @pl.when(pl.program_id(0) == 0)
    def _():
        carry_ref[...] = jnp.zeros(carry_ref.shape, F32)

    def wcols(c0):
        if c0 < CONV_BLOCK:
            return w0_ref[:, c0:c0 + cb]
        return w1_ref[:, c0 - CONV_BLOCK:c0 - CONV_BLOCK + cb]

    for k in range(CONV_WIDTH // cb):
        cols = slice(k * cb, (k + 1) * cb)
        gk = gs_ref.at[k % 2]
        a = _dot(h_ref[...], wcols(k * cb))
        b = _dot(h_ref[...], wcols(CONV_WIDTH + k * cb))
        z = _dot(h_ref[...], wcols(2 * CONV_WIDTH + k * cb))
        g = a * _sigmoid(b)
        gk[0, 0:CONV_HALO, :] = jnp.where(ti == 0, 0.0, carry_ref[:, cols])
        gk[0, CONV_HALO:, :] = g
        carry_ref[:, cols] = g[tm - CONV_HALO:, :]
        zs_ref[:, cols] = _silu(z)
        for s in range(1, SUBLANES):
            gk[s, 0:ext - SUBLANES, :] = gk[0, s:s + ext - SUBLANES, :]
        for rc in range(tm // rows):
            acc = jnp.zeros((rows, cb), F32)
            for j in range(CONV_KERNEL):
                q, s = divmod(base + j, SUBLANES)
                r0 = rc * rows + q * SUBLANES
                acc = acc + dw_ref[j:j + 1, cols] * gk[s, r0:r0 + rows, :]
            yscr_ref[rc * rows:(rc + 1) * rows, cols] = acc + dwb_ref[:, cols]
        _gate_step(h_ref, (wg0, wg1, wg2, wg3)[k], gate_ref, k)
    for rb in range(tm // tail_rows):
        rs = slice(rb * tail_rows, (rb + 1) * tail_rows)
        y = yscr_ref[rs, :]
        mu = jnp.mean(y, axis=-1, keepdims=True)
        yc = y - mu
        var = jnp.mean(yc * yc, axis=-1, keepdims=True)
        yn = yc * lax.rsqrt(var + EPS) * lg_ref[...] + lb_ref[...]
        out = _dot(_silu(yn).astype(BF16), pw_ref[...])
        y_ref[rs, :] = (out * zs_ref[rs, :]).astype(BF16)


def _branch_specs(m, d, layer, gate_index, tm=BRANCH_TM):
    once = pl.Buffered(1)

    def w_spec(width, col):
        assert col % width == 0
        return pl.BlockSpec((d, width), lambda i: (0, col // width), pipeline_mode=once)

    def row_spec(width):
        return pl.BlockSpec((tm, width), lambda i: (i, 0))

    gate_col = GATE_COL + gate_index * D_MODEL
    gate_w_specs = [w_spec(GATE_BLOCK, gate_col + k * GATE_BLOCK) for k in range(D_MODEL // GATE_BLOCK)]
    out_shape = (jax.ShapeDtypeStruct((m, POOL_WIDTH), BF16), jax.ShapeDtypeStruct((m, d), BF16))
    params = pltpu.CompilerParams(dimension_semantics=("arbitrary",), vmem_limit_bytes=VMEM_LIMIT)
    return tm, once, w_spec, row_spec, gate_w_specs, out_shape, params


def _pool_branch(h, w_in, pool_w, pool_scale, layer, seq, next_weights):
    m, d = h.shape
    tm, once, w_spec, row_spec, gate_w_specs, out_shape, params = _branch_specs(m, d, layer, 0, POOL_TM)
    steps = m // tm
    src_specs, dst_specs, dst_shapes = [], [], []
    for w in next_weights:
        rows, cols = w.shape[1:]
        rb = rows // steps
        assert rows % steps == 0 and rb % BF16_ROWS == 0
        src_specs.append(pl.BlockSpec((None, rb, cols), lambda i: (layer + 1, i, 0)))
        dst_specs.append(pl.BlockSpec((rb, cols), lambda i: (i, 0)))
        dst_shapes.append(jax.ShapeDtypeStruct((rows, cols), BF16))
    return pl.pallas_call(
        functools.partial(_pool_branch_kernel, tiles_per_seq=seq // tm, n_cast=len(next_weights)),
        out_shape=(*out_shape, *dst_shapes),
        grid=(steps,),
        in_specs=[row_spec(d), w_spec(2 * POOL_WIDTH, POOL_COL)] + gate_w_specs + [
            pl.BlockSpec((None, N_POOL_GROUPS, POOL_GROUP, POOL_GROUP), lambda i: (layer, 0, 0, 0),
                         pipeline_mode=once),
            pl.BlockSpec((None, 1, POOL_WIDTH), lambda i: (layer, 0, 0)),
        ] + src_specs,
        out_specs=(row_spec(POOL_WIDTH), row_spec(d), *dst_specs),
        scratch_shapes=[pltpu.VMEM((POOL_HALO, POOL_WIDTH), F32)],
        compiler_params=params,
        name="pool_branch",
    )(h, w_in, w_in, w_in, w_in, w_in, pool_w, pool_scale, *next_weights)


def _attn_branch(h, w_in, attn_sink, layer, seq):
    m, d = h.shape
    tm, once, w_spec, row_spec, gate_w_specs, out_shape, params = _branch_specs(m, d, layer, 1)
    half = ATTN_WIDTH // 2
    return pl.pallas_call(
        functools.partial(_attn_branch_kernel, tiles_per_seq=seq // tm, layer=layer),
        out_shape=out_shape,
        grid=(m // tm,),
        in_specs=[pl.BlockSpec(memory_space=pltpu.SMEM), row_spec(d),
                  w_spec(ATTN_WIDTH, Q_COL), w_spec(2 * KV_WIDTH, KV_COL),
                  w_spec(half, ATTN_Z_COL), w_spec(half, ATTN_Z_COL + half)] + gate_w_specs,
        out_specs=(row_spec(ATTN_WIDTH), row_spec(d)),
        scratch_shapes=[pltpu.VMEM((tm + ATTN_HALO, 2 * KV_WIDTH), BF16)],
        compiler_params=params,
        name="attn_branch",
    )(attn_sink, h, w_in, w_in, w_in, w_in, w_in, w_in, w_in, w_in)


def _conv_branch(h, w_in, conv_dw, conv_dw_b, conv_ln_g, conv_ln_b, conv_pw, layer, seq):
    m, d = h.shape
    tm, once, w_spec, row_spec, gate_w_specs, out_shape, params = _branch_specs(m, d, layer, 2)
    vec = pl.BlockSpec((None, 1, CONV_WIDTH), lambda i: (layer, 0, 0))
    return pl.pallas_call(
        functools.partial(_conv_branch_kernel, tiles_per_seq=seq // tm, rows=64, cb=CONV_CHAIN,
                          tail_rows=256),
        out_shape=out_shape,
        grid=(m // tm,),
        in_specs=[row_spec(d), w_spec(CONV_BLOCK, CONV_COL), w_spec(CONV_BLOCK, CONV_COL + CONV_BLOCK)]
        + gate_w_specs + [
            pl.BlockSpec((None, CONV_KERNEL, CONV_WIDTH), lambda i: (layer, 0, 0)),
            vec, vec, vec,
            pl.BlockSpec((None, CONV_WIDTH, CONV_WIDTH), lambda i: (layer, 0, 0), pipeline_mode=once),
        ],
        out_specs=(row_spec(CONV_WIDTH), row_spec(d)),
        scratch_shapes=[pltpu.VMEM((2, SUBLANES, tm + CONV_HALO, CONV_CHAIN), F32),
                        pltpu.VMEM((CONV_HALO, CONV_WIDTH), F32),
                        pltpu.VMEM((tm, CONV_WIDTH), F32),
                        pltpu.VMEM((tm, CONV_WIDTH), F32)],
        compiler_params=params,
        name="conv_branch",
    )(h, w_in, w_in, w_in, w_in, w_in, w_in, conv_dw, conv_dw_b, conv_ln_g, conv_ln_b, conv_pw)


def _merged_update(x_ref, yp_ref, ya_ref, yc_ref, gp_ref, ga_ref, gc_ref, wp_ref, wa_ref, wc_ref,
                   wo_ref, gate_ref):
    def branch(y_ref, g_ref, w_ref):
        return g_ref[...].astype(F32) * _dot(y_ref[...], w_ref[...])

    merged = branch(yp_ref, gp_ref, wp_ref) + branch(ya_ref, ga_ref, wa_ref) + branch(yc_ref, gc_ref, wc_ref)
    return x_ref[...] + gate_ref[...] * _dot(merged.astype(BF16), wo_ref[...])


def _merge_kernel(x_ref, yp_ref, ya_ref, yc_ref, gp_ref, ga_ref, gc_ref, wp_ref, wa_ref, wc_ref,
                  wo_ref, gate_ref, ng_ref, nsc_ref, nsh_ref, wsrc_ref, o_ref, h_ref, wdst_ref,
                  *, cast_steps):
    xn = _merged_update(x_ref, yp_ref, ya_ref, yc_ref, gp_ref, ga_ref, gc_ref, wp_ref, wa_ref, wc_ref,
                        wo_ref, gate_ref)
    o_ref[...] = xn
    h_ref[...] = _ada_norm(xn, ng_ref[...], nsc_ref[...], nsh_ref[...]).astype(BF16)

    @pl.when(pl.program_id(0) < cast_steps)
    def _():
        wdst_ref[...] = wsrc_ref[...].astype(BF16)


def _merge_last_kernel(x_ref, yp_ref, ya_ref, yc_ref, gp_ref, ga_ref, gc_ref, wp_ref, wa_ref, wc_ref,
                       wo_ref, gate_ref, fg_ref, o_ref):
    xn = _merged_update(x_ref, yp_ref, ya_ref, yc_ref, gp_ref, ga_ref, gc_ref, wp_ref, wa_ref, wc_ref,
                        wo_ref, gate_ref)
    ms = jnp.mean(xn * xn, axis=-1, keepdims=True)
    o_ref[...] = xn * lax.rsqrt(ms + EPS) * fg_ref[...]


def _merge_out(xf, ys, gates, w_bp, w_ba, w_bc, w_out, gate, layer, seq, nxt):
    m, d = xf.shape
    tm = MERGE_TM
    once = pl.Buffered(1)
    row = pl.BlockSpec((tm, d), lambda i: (i, 0))
    ybs = pl.BlockSpec((tm, POOL_WIDTH), lambda i: (i, 0))
    wbs = pl.BlockSpec((POOL_WIDTH, d), lambda i: (0, 0), pipeline_mode=once)
    per_batch = pl.BlockSpec((None, 1, d), lambda i: ((i * tm) // seq, 0, 0))
    in_specs = [row, ybs, ybs, ybs, row, row, row, wbs, wbs, wbs,
                pl.BlockSpec((d, d), lambda i: (0, 0), pipeline_mode=once), per_batch]
    args = [xf, *ys, *gates, w_bp, w_ba, w_bc, w_out, gate]
    params = pltpu.CompilerParams(dimension_semantics=("parallel",), vmem_limit_bytes=VMEM_LIMIT)
    if len(nxt) == 1:
        return pl.pallas_call(
            _merge_last_kernel,
            out_shape=jax.ShapeDtypeStruct((m, d), F32),
            grid=(m // tm,),
            in_specs=in_specs + [pl.BlockSpec((1, d), lambda i: (0, 0))],
            out_specs=row,
            compiler_params=params,
            name="merge_last",
        )(*args, *nxt)
    norm_g, scale, shift, w_in = nxt
    n_in = w_in.shape[-1]
    cast_steps = n_in // CAST_BLOCK
    assert n_in % CAST_BLOCK == 0 and cast_steps <= m // tm
    cast_col = lambda i: jnp.minimum(i, cast_steps - 1)
    return pl.pallas_call(
        functools.partial(_merge_kernel, cast_steps=cast_steps),
        out_shape=(jax.ShapeDtypeStruct((m, d), F32), jax.ShapeDtypeStruct((m, d), BF16),
                   jax.ShapeDtypeStruct((d, n_in), BF16)),
        grid=(m // tm,),
        in_specs=in_specs + [pl.BlockSpec((None, 1, d), lambda i: (layer + 1, 0, 0)), per_batch, per_batch,
                             pl.BlockSpec((None, d, CAST_BLOCK), lambda i: (layer + 1, 0, cast_col(i)))],
        out_specs=(row, row, pl.BlockSpec((d, CAST_BLOCK), lambda i: (0, cast_col(i)))),
        compiler_params=pltpu.CompilerParams(dimension_semantics=("arbitrary",), vmem_limit_bytes=VMEM_LIMIT),
        name="merge_out",
    )(*args, norm_g, scale, shift, w_in)


def kernel(x, c, norm_g, w_ada, b_ada, w_in, pool_w, pool_scale, attn_sink, conv_dw, conv_dw_b,
           conv_ln_g, conv_ln_b, conv_pw, w_branch_pool, w_branch_attn, w_branch_conv, w_out,
           final_g):
    batch, seq, d = x.shape
    depth = w_in.shape[0]
    m = batch * seq
    assert d == D_MODEL and w_in.shape[-1] == IN_WIDTH and batch <= 8
    assert seq % BRANCH_TM == 0 and seq % MERGE_TM == 0 and seq % POOL_TM == 0

    w_in_b = w_in[0].astype(BF16)
    pool_w_b = pool_w.astype(BF16)
    conv_pw_b = conv_pw.astype(BF16)
    merge_w = (w_branch_pool, w_branch_attn, w_branch_conv, w_out)
    merge_w_b = tuple(w[0].astype(BF16) for w in merge_w)

    c_pad = jnp.zeros((8, d), F32).at[:batch].set(c)
    mod = _ada_mod(c_pad, w_ada, b_ada)
    shifts = [mod[l, :batch, 0:d].reshape(batch, 1, d) for l in range(depth)]
    scales = [mod[l, :batch, d:2 * d].reshape(batch, 1, d) for l in range(depth)]
    gates = [mod[l, :batch, 2 * d:3 * d].reshape(batch, 1, d) for l in range(depth)]

    norm_g3 = norm_g.reshape(depth, 1, d)
    pool_scale3 = pool_scale.reshape(depth, 1, POOL_WIDTH)
    dwb3 = conv_dw_b.reshape(depth, 1, CONV_WIDTH)
    lng3 = conv_ln_g.reshape(depth, 1, CONV_WIDTH)
    lnb3 = conv_ln_b.reshape(depth, 1, CONV_WIDTH)

    xf = x.reshape(m, d)
    h = _first_norm(xf, norm_g3, scales[0], shifts[0], seq)
    for layer in range(depth):
        last = layer + 1 == depth
        y_pool, g_pool, *next_merge_w = _pool_branch(h, w_in_b, pool_w_b, pool_scale3, layer, seq,
                                                     () if last else merge_w)
        y_attn, g_attn = _attn_branch(h, w_in_b, attn_sink, layer, seq)
        y_conv, g_conv = _conv_branch(h, w_in_b, conv_dw, dwb3, lng3, lnb3, conv_pw_b, layer, seq)
        if last:
            nxt = (final_g.reshape(1, d),)
        else:
            nxt = (norm_g3, scales[layer + 1], shifts[layer + 1], w_in)
        res = _merge_out(xf, (y_pool, y_attn, y_conv), (g_pool, g_attn, g_conv), *merge_w_b,
                         gates[layer], layer, seq, nxt)
        if last:
            xf = res
        else:
            xf, h, w_in_b = res
            merge_w_b = tuple(next_merge_w)
    return xf.reshape(batch, seq, d)
```

```python
import functools

import jax
import jax.numpy as jnp
from jax import lax
from jax.experimental import pallas as pl
from jax.experimental.pallas import tpu as pltpu

D_MODEL = 2048
CHUNK = 64
EPS = 1e-6
POOL_WIDTH = 1024
POOL_WINDOWS = (2, 4, 8, 16)
N_POOL_GROUPS = 4
POOL_GROUP = POOL_WIDTH // N_POOL_GROUPS
N_Q_HEADS = 16
N_KV_HEADS = 4
HEAD_DIM = 64
GROUP = N_Q_HEADS // N_KV_HEADS
ATTN_WIDTH = N_Q_HEADS * HEAD_DIM
KV_WIDTH = N_KV_HEADS * HEAD_DIM
WINDOW_CHUNKS = 2
N_KEYS = (WINDOW_CHUNKS + 1) * CHUNK
CONV_WIDTH = 1024
CONV_KERNEL = 31
IN_WIDTH = 2 * POOL_WIDTH + 2 * ATTN_WIDTH + 2 * KV_WIDTH + 3 * CONV_WIDTH + 3 * D_MODEL

POOL_COL = 0
Q_COL = 2 * POOL_WIDTH
KV_COL = Q_COL + ATTN_WIDTH
ATTN_Z_COL = KV_COL + 2 * KV_WIDTH
CONV_COL = ATTN_Z_COL + ATTN_WIDTH
GATE_COL = CONV_COL + 3 * CONV_WIDTH
GATE_BLOCK = 512
CONV_BLOCK = 1536
CONV_CHAIN = 256
CAST_BLOCK = 512

POOL_HALO = 16
CONV_HALO = 32
ATTN_HALO = WINDOW_CHUNKS * CHUNK
SUBLANES = 8
BF16_ROWS = 16

BRANCH_TM = 512
POOL_TM = 1024
MERGE_TM = 256
NORM_TM = 512
ADA_TN = 1024
VMEM_LIMIT = 56 * 1024 * 1024

F32 = jnp.float32
BF16 = jnp.bfloat16


def _sigmoid(v):
    return jax.nn.sigmoid(v)


def _silu(v):
    return v * jax.nn.sigmoid(v)


def _dot(a, b):
    return jnp.dot(a, b, preferred_element_type=F32)


def _ada_norm(x, g, scale, shift):
    ms = jnp.mean(x * x, axis=-1, keepdims=True)
    return x * lax.rsqrt(ms + EPS) * g * (1.0 + scale) + shift


def _ada_kernel(c_ref, w_ref, b_ref, o_ref, *, batch):
    ca = _silu(c_ref[...])
    w = w_ref[...]
    rows = [jnp.sum(w * ca[:, r:r + 1], axis=0, keepdims=True) for r in range(batch)]
    if batch < o_ref.shape[0]:
        rows.append(jnp.zeros((o_ref.shape[0] - batch, w.shape[1]), F32))
    o_ref[...] = jnp.concatenate(rows, axis=0) + b_ref[...]


def _ada_mod(c_pad, w_ada, b_ada, batch):
    depth, d, n = w_ada.shape
    tn = ADA_TN
    return pl.pallas_call(
        functools.partial(_ada_kernel, batch=batch),
        out_shape=jax.ShapeDtypeStruct((depth, c_pad.shape[0], n), F32),
        grid=(depth, n // tn),
        in_specs=[
            pl.BlockSpec(c_pad.shape[::-1], lambda l, j: (0, 0)),
            pl.BlockSpec((None, d, tn), lambda l, j: (l, 0, j)),
            pl.BlockSpec((None, 1, tn), lambda l, j: (l, 0, j)),
        ],
        out_specs=pl.BlockSpec((None, c_pad.shape[0], tn), lambda l, j: (l, 0, j)),
        compiler_params=pltpu.CompilerParams(dimension_semantics=("parallel", "parallel")),
        name="ada_mod",
    )(c_pad.T, w_ada, b_ada.reshape(depth, 1, n))


def _norm_kernel(x_ref, g_ref, sc_ref, sh_ref, h_ref):
    h_ref[...] = _ada_norm(x_ref[...], g_ref[...], sc_ref[...], sh_ref[...]).astype(BF16)


def _first_norm(xf, norm_g, scale, shift, seq):
    m, d = xf.shape
    tm = NORM_TM
    vec = lambda i: ((i * tm) // seq, 0, 0)
    return pl.pallas_call(
        _norm_kernel,
        out_shape=jax.ShapeDtypeStruct((m, d), BF16),
        grid=(m // tm,),
        in_specs=[pl.BlockSpec((tm, d), lambda i: (i, 0)),
                  pl.BlockSpec((None, 1, d), lambda i: (0, 0, 0)),
                  pl.BlockSpec((None, 1, d), vec),
                  pl.BlockSpec((None, 1, d), vec)],
        out_specs=pl.BlockSpec((tm, d), lambda i: (i, 0)),
        compiler_params=pltpu.CompilerParams(dimension_semantics=("parallel",)),
        name="first_norm",
    )(xf, norm_g, scale, shift)


def _gate_step(h_ref, wg_ref, gate_ref, k):
    gate_ref[:, k * GATE_BLOCK:(k + 1) * GATE_BLOCK] = _sigmoid(_dot(h_ref[...], wg_ref[...])).astype(BF16)


def _pool_branch_kernel(h_ref, wp_ref, wg0, wg1, wg2, wg3, pw_ref, ps_ref, *rest, tiles_per_seq, n_cast):
    src_refs = rest[:n_cast]
    y_ref, gate_ref = rest[n_cast:n_cast + 2]
    dst_refs = rest[n_cast + 2:2 * n_cast + 2]
    carry_ref = rest[2 * n_cast + 2]
    tm = h_ref.shape[0]
    ti = pl.program_id(0) % tiles_per_seq
    for src_ref, dst_ref in zip(src_refs, dst_refs):
        dst_ref[...] = src_ref[...].astype(BF16)

    @pl.when(pl.program_id(0) == 0)
    def _():
        carry_ref[...] = jnp.zeros(carry_ref.shape, F32)

    t = ti * tm + lax.broadcasted_iota(jnp.int32, (tm, POOL_GROUP), 0)
    for gi, w in enumerate(POOL_WINDOWS):
        cols = slice(gi * POOL_GROUP, (gi + 1) * POOL_GROUP)
        zcols = slice(POOL_WIDTH + gi * POOL_GROUP, POOL_WIDTH + (gi + 1) * POOL_GROUP)
        ug = _dot(h_ref[...], wp_ref[:, cols])
        zg = _dot(h_ref[...], wp_ref[:, zcols])
        halo = jnp.where(ti == 0, 0.0, carry_ref[:, cols])
        carry_ref[:, cols] = ug[tm - POOL_HALO:, :]
        s = jnp.concatenate([halo, ug], axis=0)
        k = 1
        while k < w:
            s = s[k:] + s[:-k]
            k *= 2
        off = POOL_HALO - (w - 1)
        s = s[off:off + tm]
        cnt = jnp.minimum(t + 1, w).astype(F32)
        mixed = (s / cnt - ug).astype(BF16)
        y = _dot(mixed, pw_ref[gi]) * ps_ref[:, cols] * _silu(zg)
        y_ref[:, cols] = y.astype(BF16)
        _gate_step(h_ref, (wg0, wg1, wg2, wg3)[gi], gate_ref, gi)


def _attn_branch_kernel(sink_ref, h_ref, wq_ref, wkv_ref, wz0_ref, wz1_ref, wg0, wg1, wg2, wg3,
                        y_ref, gate_ref, kvx_ref, *, tiles_per_seq, layer):
    tm = h_ref.shape[0]
    ti = pl.program_id(0) % tiles_per_seq
    head_cols = GROUP * HEAD_DIM

    @pl.when(pl.program_id(0) == 0)
    def _():
        kvx_ref[0:ATTN_HALO, :] = jnp.zeros((ATTN_HALO, 2 * KV_WIDTH), BF16)

    kvx_ref[ATTN_HALO:, :] = _dot(h_ref[...], wkv_ref[...]).astype(BF16)
    row_head = lax.broadcasted_iota(jnp.int32, (GROUP * CHUNK, 1), 0) // CHUNK
    key_chunk = lax.broadcasted_iota(jnp.int32, (1, N_KEYS), 1) // CHUNK
    for kvh in range(N_KV_HEADS):
        cols = slice(kvh * head_cols, (kvh + 1) * head_cols)
        q = _dot(h_ref[...], wq_ref[:, cols]).astype(BF16)
        wz_ref = wz0_ref if kvh < N_KV_HEADS // 2 else wz1_ref
        zoff = (kvh % (N_KV_HEADS // 2)) * head_cols
        sz = _silu(_dot(h_ref[...], wz_ref[:, zoff:zoff + head_cols]))
        heads = [kvh * GROUP + g for g in range(GROUP)]
        sk = jnp.full((GROUP * CHUNK, 1), sink_ref[layer, heads[GROUP - 1]], F32)
        for g in range(GROUP - 2, -1, -1):
            sk = jnp.where(row_head == g, sink_ref[layer, heads[g]], sk)
        for c in range(tm // CHUNK):
            r0 = c * CHUNK
            qc = q[r0:r0 + CHUNK, :]
            qs = jnp.concatenate([qc[:, g * HEAD_DIM:(g + 1) * HEAD_DIM] for g in range(GROUP)], axis=0)
            kk = kvx_ref[r0:r0 + N_KEYS, kvh * HEAD_DIM:(kvh + 1) * HEAD_DIM]
            vv = kvx_ref[r0:r0 + N_KEYS, KV_WIDTH + kvh * HEAD_DIM:KV_WIDTH + (kvh + 1) * HEAD_DIM]
            n = ti * (tm // CHUNK) + c
            bias = jnp.where(n - WINDOW_CHUNKS + key_chunk >= 0, 0.0, -jnp.inf).astype(F32)
            s = lax.dot_general(qs, kk, (((1,), (1,)), ((), ())), preferred_element_type=F32)
            s = s * (HEAD_DIM ** -0.5) + bias
            mx = jnp.maximum(jnp.max(s, axis=-1, keepdims=True), sk)
            p = jnp.exp(s - mx)
            denom = jnp.sum(p, axis=-1, keepdims=True) + jnp.exp(sk - mx)
            o = _dot(p.astype(BF16), vv) / denom
            oc = jnp.concatenate([o[g * CHUNK:(g + 1) * CHUNK] for g in range(GROUP)], axis=1)
            y_ref[r0:r0 + CHUNK, cols] = (oc * sz[r0:r0 + CHUNK, :]).astype(BF16)
        _gate_step(h_ref, (wg0, wg1, wg2, wg3)[kvh], gate_ref, kvh)
    kvx_ref[0:ATTN_HALO, :] = kvx_ref[tm:tm + ATTN_HALO, :]


def _conv_branch_kernel(h_ref, w0_ref, w1_ref, wg0, wg1, wg2, wg3, dw_ref, dwb_ref, lg_ref, lb_ref,
                        pw_ref, y_ref, gate_ref, gs_ref, carry_ref, yscr_ref, zs_ref,
                        *, tiles_per_seq, rows, cb, tail_rows):
    tm = h_ref.shape[0]
    ext = tm + CONV_HALO
    base = CONV_HALO - (CONV_KERNEL - 1)
    ti = pl.program_id(0) % tiles_per_seq

    @pl.when(pl.program_id(0) == 0)
    def _():
        carry_ref[...] = jnp.zeros(carry_ref.shape, F32)

    def wcols(c0):
        if c0 < CONV_BLOCK:
            return w0_ref[:, c0:c0 + cb]
        return w1_ref[:, c0 - CONV_BLOCK:c0 - CONV_BLOCK + cb]

    for k in range(CONV_WIDTH // cb):
        cols = slice(k * cb, (k + 1) * cb)
        gk = gs_ref.at[k % 2]
        a = _dot(h_ref[...], wcols(k * cb))
        b = _dot(h_ref[...], wcols(CONV_WIDTH + k * cb))
        z = _dot(h_ref[...], wcols(2 * CONV_WIDTH + k * cb))
        g = a * _sigmoid(b)
        gk[0, 0:CONV_HALO, :] = jnp.where(ti == 0, 0.0, carry_ref[:, cols])
        gk[0, CONV_HALO:, :] = g
        carry_ref[:, cols] = g[tm - CONV_HALO:, :]
        zs_ref[:, cols] = _silu(z)
        for s in range(1, SUBLANES):
            gk[s, 0:ext - SUBLANES, :] = gk[0, s:s + ext - SUBLANES, :]
        for rc in range(tm // rows):
            acc = jnp.zeros((rows, cb), F32)
            for j in range(CONV_KERNEL):
                q, s = divmod(base + j, SUBLANES)
                r0 = rc * rows + q * SUBLANES
                acc = acc + dw_ref[j:j + 1, cols] * gk[s, r0:r0 + rows, :]
            yscr_ref[rc * rows:(rc + 1) * rows, cols] = acc + dwb_ref[:, cols]
        _gate_step(h_ref, (wg0, wg1, wg2, wg3)[k], gate_ref, k)
    for rb in range(tm // tail_rows):
        rs = slice(rb * tail_rows, (rb + 1) * tail_rows)
        y = yscr_ref[rs, :]
        mu = jnp.mean(y, axis=-1, keepdims=True)
        yc = y - mu
        var = jnp.mean(yc * yc, axis=-1, keepdims=True)
        yn = yc * lax.rsqrt(var + EPS) * lg_ref[...] + lb_ref[...]
        out = _dot(_silu(yn).astype(BF16), pw_ref[...])
        y_ref[rs, :] = (out * zs_ref[rs, :]).astype(BF16)


def _branch_specs(m, d, layer, gate_index, tm=BRANCH_TM):
    once = pl.Buffered(1)

    def w_spec(width, col):
        assert col % width == 0
        return pl.BlockSpec((d, width), lambda i: (0, col // width), pipeline_mode=once)

    def row_spec(width):
        return pl.BlockSpec((tm, width), lambda i: (i, 0))

    gate_col = GATE_COL + gate_index * D_MODEL
    gate_w_specs = [w_spec(GATE_BLOCK, gate_col + k * GATE_BLOCK) for k in range(D_MODEL // GATE_BLOCK)]
    out_shape = (jax.ShapeDtypeStruct((m, POOL_WIDTH), BF16), jax.ShapeDtypeStruct((m, d), BF16))
    params = pltpu.CompilerParams(dimension_semantics=("arbitrary",), vmem_limit_bytes=VMEM_LIMIT)
    return tm, once, w_spec, row_spec, gate_w_specs, out_shape, params


def _pool_branch(h, w_in, pool_w, pool_scale, layer, seq, next_weights):
    m, d = h.shape
    tm, once, w_spec, row_spec, gate_w_specs, out_shape, params = _branch_specs(m, d, layer, 0, POOL_TM)
    steps = m // tm
    src_specs, dst_specs, dst_shapes = [], [], []
    for w in next_weights:
        rows, cols = w.shape[1:]
        rb = rows // steps
        assert rows % steps == 0 and rb % BF16_ROWS == 0
        src_specs.append(pl.BlockSpec((None, rb, cols), lambda i: (layer + 1, i, 0)))
        dst_specs.append(pl.BlockSpec((rb, cols), lambda i: (i, 0)))
        dst_shapes.append(jax.ShapeDtypeStruct((rows, cols), BF16))
    return pl.pallas_call(
        functools.partial(_pool_branch_kernel, tiles_per_seq=seq // tm, n_cast=len(next_weights)),
        out_shape=(*out_shape, *dst_shapes),
        grid=(steps,),
        in_specs=[row_spec(d), w_spec(2 * POOL_WIDTH, POOL_COL)] + gate_w_specs + [
            pl.BlockSpec((None, N_POOL_GROUPS, POOL_GROUP, POOL_GROUP), lambda i: (layer, 0, 0, 0),
                         pipeline_mode=once),
            pl.BlockSpec((None, 1, POOL_WIDTH), lambda i: (layer, 0, 0)),
        ] + src_specs,
        out_specs=(row_spec(POOL_WIDTH), row_spec(d), *dst_specs),
        scratch_shapes=[pltpu.VMEM((POOL_HALO, POOL_WIDTH), F32)],
        compiler_params=params,
        name="pool_branch",
    )(h, w_in, w_in, w_in, w_in, w_in, pool_w, pool_scale, *next_weights)


def _attn_branch(h, w_in, attn_sink, layer, seq):
    m, d = h.shape
    tm, once, w_spec, row_spec, gate_w_specs, out_shape, params = _branch_specs(m, d, layer, 1)
    half = ATTN_WIDTH // 2
    return pl.pallas_call(
        functools.partial(_attn_branch_kernel, tiles_per_seq=seq // tm, layer=layer),
        out_shape=out_shape,
        grid=(m // tm,),
        in_specs=[pl.BlockSpec(memory_space=pltpu.SMEM), row_spec(d),
                  w_spec(ATTN_WIDTH, Q_COL), w_spec(2 * KV_WIDTH, KV_COL),
                  w_spec(half, ATTN_Z_COL), w_spec(half, ATTN_Z_COL + half)] + gate_w_specs,
        out_specs=(row_spec(ATTN_WIDTH), row_spec(d)),
        scratch_shapes=[pltpu.VMEM((tm + ATTN_HALO, 2 * KV_WIDTH), BF16)],
        compiler_params=params,
        name="attn_branch",
    )(attn_sink, h, w_in, w_in, w_in, w_in, w_in, w_in, w_in, w_in)


def _conv_branch(h, w_in, conv_dw, conv_dw_b, conv_ln_g, conv_ln_b, conv_pw, layer, seq):
    m, d = h.shape
    tm, once, w_spec, row_spec, gate_w_specs, out_shape, params = _branch_specs(m, d, layer, 2)
    vec = pl.BlockSpec((None, 1, CONV_WIDTH), lambda i: (layer, 0, 0))
    return pl.pallas_call(
        functools.partial(_conv_branch_kernel, tiles_per_seq=seq // tm, rows=64, cb=CONV_CHAIN,
                          tail_rows=256),
        out_shape=out_shape,
        grid=(m // tm,),
        in_specs=[row_spec(d), w_spec(CONV_BLOCK, CONV_COL), w_spec(CONV_BLOCK, CONV_COL + CONV_BLOCK)]
        + gate_w_specs + [
            pl.BlockSpec((None, CONV_KERNEL, CONV_WIDTH), lambda i: (layer, 0, 0)),
            vec, vec, vec,
            pl.BlockSpec((None, CONV_WIDTH, CONV_WIDTH), lambda i: (layer, 0, 0), pipeline_mode=once),
        ],
        out_specs=(row_spec(CONV_WIDTH), row_spec(d)),
        scratch_shapes=[pltpu.VMEM((2, SUBLANES, tm + CONV_HALO, CONV_CHAIN), F32),
                        pltpu.VMEM((CONV_HALO, CONV_WIDTH), F32),
                        pltpu.VMEM((tm, CONV_WIDTH), F32),
                        pltpu.VMEM((tm, CONV_WIDTH), F32)],
        compiler_params=params,
        name="conv_branch",
    )(h, w_in, w_in, w_in, w_in, w_in, w_in, conv_dw, conv_dw_b, conv_ln_g, conv_ln_b, conv_pw)


def _merged_update(x_ref, yp_ref, ya_ref, yc_ref, gp_ref, ga_ref, gc_ref, wp_ref, wa_ref, wc_ref,
                   wo_ref, gate_ref):
    def branch(y_ref, g_ref, w_ref):
        return g_ref[...].astype(F32) * _dot(y_ref[...], w_ref[...])

    merged = branch(yp_ref, gp_ref, wp_ref) + branch(ya_ref, ga_ref, wa_ref) + branch(yc_ref, gc_ref, wc_ref)
    return x_ref[...] + gate_ref[...] * _dot(merged.astype(BF16), wo_ref[...])


def _merge_kernel(x_ref, yp_ref, ya_ref, yc_ref, gp_ref, ga_ref, gc_ref, wp_ref, wa_ref, wc_ref,
                  wo_ref, gate_ref, ng_ref, nsc_ref, nsh_ref, wsrc_ref, o_ref, h_ref, wdst_ref,
                  *, cast_steps):
    xn = _merged_update(x_ref, yp_ref, ya_ref, yc_ref, gp_ref, ga_ref, gc_ref, wp_ref, wa_ref, wc_ref,
                        wo_ref, gate_ref)
    o_ref[...] = xn
    h_ref[...] = _ada_norm(xn, ng_ref[...], nsc_ref[...], nsh_ref[...]).astype(BF16)

    @pl.when(pl.program_id(0) < cast_steps)
    def _():
        wdst_ref[...] = wsrc_ref[...].astype(BF16)


def _merge_last_kernel(x_ref, yp_ref, ya_ref, yc_ref, gp_ref, ga_ref, gc_ref, wp_ref, wa_ref, wc_ref,
                       wo_ref, gate_ref, fg_ref, o_ref):
    xn = _merged_update(x_ref, yp_ref, ya_ref, yc_ref, gp_ref, ga_ref, gc_ref, wp_ref, wa_ref, wc_ref,
                        wo_ref, gate_ref)
    ms = jnp.mean(xn * xn, axis=-1, keepdims=True)
    o_ref[...] = xn * lax.rsqrt(ms + EPS) * fg_ref[...]


def _merge_out(xf, ys, gates, w_bp, w_ba, w_bc, w_out, gate, layer, seq, nxt):
    m, d = xf.shape
    tm = MERGE_TM
    once = pl.Buffered(1)
    row = pl.BlockSpec((tm, d), lambda i: (i, 0))
    ybs = pl.BlockSpec((tm, POOL_WIDTH), lambda i: (i, 0))
    wbs = pl.BlockSpec((POOL_WIDTH, d), lambda i: (0, 0), pipeline_mode=once)
    per_batch = pl.BlockSpec((None, 1, d), lambda i: ((i * tm) // seq, 0, 0))
    in_specs = [row, ybs, ybs, ybs, row, row, row, wbs, wbs, wbs,
                pl.BlockSpec((d, d), lambda i: (0, 0), pipeline_mode=once), per_batch]
    args = [xf, *ys, *gates, w_bp, w_ba, w_bc, w_out, gate]
    params = pltpu.CompilerParams(dimension_semantics=("parallel",), vmem_limit_bytes=VMEM_LIMIT)
    if len(nxt) == 1:
        return pl.pallas_call(
            _merge_last_kernel,
            out_shape=jax.ShapeDtypeStruct((m, d), F32),
            grid=(m // tm,),
            in_specs=in_specs + [pl.BlockSpec((1, d), lambda i: (0, 0))],
            out_specs=row,
            compiler_params=params,
            name="merge_last",
        )(*args, *nxt)
    norm_g, scale, shift, w_in = nxt
    n_in = w_in.shape[-1]
    cast_steps = n_in // CAST_BLOCK
    assert n_in % CAST_BLOCK == 0 and cast_steps <= m // tm
    cast_col = lambda i: jnp.minimum(i, cast_steps - 1)
    return pl.pallas_call(
        functools.partial(_merge_kernel, cast_steps=cast_steps),
        out_shape=(jax.ShapeDtypeStruct((m, d), F32), jax.ShapeDtypeStruct((m, d), BF16),
                   jax.ShapeDtypeStruct((d, n_in), BF16)),
        grid=(m // tm,),
        in_specs=in_specs + [pl.BlockSpec((None, 1, d), lambda i: (layer + 1, 0, 0)), per_batch, per_batch,
                             pl.BlockSpec((None, d, CAST_BLOCK), lambda i: (layer + 1, 0, cast_col(i)))],
        out_specs=(row, row, pl.BlockSpec((d, CAST_BLOCK), lambda i: (0, cast_col(i)))),
        compiler_params=pltpu.CompilerParams(dimension_semantics=("arbitrary",), vmem_limit_bytes=VMEM_LIMIT),
        name="merge_out",
    )(*args, norm_g, scale, shift, w_in)


def kernel(x, c, norm_g, w_ada, b_ada, w_in, pool_w, pool_scale, attn_sink, conv_dw, conv_dw_b,
           conv_ln_g, conv_ln_b, conv_pw, w_branch_pool, w_branch_attn, w_branch_conv, w_out,
           final_g):
    batch, seq, d = x.shape
    depth = w_in.shape[0]
    m = batch * seq
    assert d == D_MODEL and w_in.shape[-1] == IN_WIDTH and batch <= 8
    assert seq % BRANCH_TM == 0 and seq % MERGE_TM == 0 and seq % POOL_TM == 0

    w_in_b = w_in[0].astype(BF16)
    pool_w_b = pool_w.astype(BF16)
    conv_pw_b = conv_pw.astype(BF16)
    merge_w = (w_branch_pool, w_branch_attn, w_branch_conv, w_out)
    merge_w_b = tuple(w[0].astype(BF16) for w in merge_w)

    c_pad = jnp.zeros((8, d), F32).at[:batch].set(c)
    mod = _ada_mod(c_pad, w_ada, b_ada, batch)
    shifts = [mod[l, :batch, 0:d].reshape(batch, 1, d) for l in range(depth)]
    scales = [mod[l, :batch, d:2 * d].reshape(batch, 1, d) for l in range(depth)]
    gates = [mod[l, :batch, 2 * d:3 * d].reshape(batch, 1, d) for l in range(depth)]

    norm_g3 = norm_g.reshape(depth, 1, d)
    pool_scale3 = pool_scale.reshape(depth, 1, POOL_WIDTH)
    dwb3 = conv_dw_b.reshape(depth, 1, CONV_WIDTH)
    lng3 = conv_ln_g.reshape(depth, 1, CONV_WIDTH)
    lnb3 = conv_ln_b.reshape(depth, 1, CONV_WIDTH)

    xf = x.reshape(m, d)
    h = _first_norm(xf, norm_g3, scales[0], shifts[0], seq)
    for layer in range(depth):
        last = layer + 1 == depth
        y_pool, g_pool, *next_merge_w = _pool_branch(h, w_in_b, pool_w_b, pool_scale3, layer, seq,
                                                     () if last else merge_w)
        y_attn, g_attn = _attn_branch(h, w_in_b, attn_sink, layer, seq)
        y_conv, g_conv = _conv_branch(h, w_in_b, conv_dw, dwb3, lng3, lnb3, conv_pw_b, layer, seq)
        if last:
            nxt = (final_g.reshape(1, d),)
        else:
            nxt = (norm_g3, scales[layer + 1], shifts[layer + 1], w_in)
        res = _merge_out(xf, (y_pool, y_attn, y_conv), (g_pool, g_attn, g_conv), *merge_w_b,
                         gates[layer], layer, seq, nxt)
        if last:
            xf = res
        else:
            xf, h, w_in_b = res
            merge_w_b = tuple(next_merge_w)
    return xf.reshape(batch, seq, d)
```
